```python
import jax, jax.numpy as jnp
from jax import lax
import numpy as np


D_MODEL = 1024
BATCH = 8
SEQ = 4096
DEPTH = 2

HEAD_DIM = 64
RET_HEADS = 6
MOBA_HEADS = 6
GMLP_GROUPS = 4
RET_WIDTH = RET_HEADS * HEAD_DIM
MOBA_WIDTH = MOBA_HEADS * HEAD_DIM
GMLP_WIDTH = GMLP_GROUPS * HEAD_DIM
MIX_WIDTH = RET_WIDTH + MOBA_WIDTH + GMLP_WIDTH
IN_WIDTH = 4 * RET_WIDTH + 3 * MOBA_WIDTH + 2 * GMLP_WIDTH
RET_CHUNK = 128
MOBA_BLOCK = 256
MOBA_TOPK = 3
MOBA_QBLOCK = 128
GMLP_CHUNK = 128
D_FF = 3584
N_EXPERTS = 8
TOP_K_EXPERTS = 2
NORM_EPS = 1e-6
GN_EPS = 1e-5

kernel_name = 'hybrid_retention_moba_gmlp_moe'

F32 = jnp.float32


def rmsnorm(x, g):
    xf = x.astype(F32)
    y = xf * lax.rsqrt(jnp.mean(xf * xf, axis=-1, keepdims=True) + NORM_EPS)
    return (y * g.astype(F32)).astype(x.dtype)


def retention(q, k, v):
    B, S, H, D = q.shape
    C = RET_CHUNK
    N = S // C
    dt = q.dtype
    log_g = jnp.log(1.0 - 2.0 ** (-5.0 - jnp.arange(H, dtype=F32)))
    pos = jnp.arange(C, dtype=F32)
    diff = pos[:, None] - pos[None, :]
    decay = jnp.where(diff >= 0, jnp.exp(log_g[:, None, None] * jnp.maximum(diff, 0.0)), 0.0)
    zeta = jnp.exp(log_g[:, None] * (C - 1 - pos))
    xi = jnp.exp(log_g[:, None] * (pos + 1))
    chunk_decay = jnp.exp(log_g * C)

    def to_chunks(t):
        return t.astype(F32).reshape(B, N, C, H, D).transpose(0, 3, 1, 2, 4)

    qc = to_chunks(q)
    kc = to_chunks(k) * (D ** -0.5)
    vc = to_chunks(v)
    scores = jnp.einsum('bhnid,bhnjd->bhnij', qc, kc) * decay[None, :, None]
    intra = jnp.einsum('bhnij,bhnjd->bhnid', scores, vc)
    kv = jnp.einsum('bhncd,bhnce->nbhde', kc * zeta[None, :, None, :, None], vc)

    def step(state, kv_n):
        return state * chunk_decay[None, :, None, None] + kv_n, state

    _, prev = lax.scan(step, jnp.zeros((B, H, D, D), F32), kv)
    cross = jnp.einsum('bhncd,nbhde->bhnce', qc * xi[None, :, None, :, None], prev)
    o = intra + cross
    mu = jnp.mean(o, axis=-1, keepdims=True)
    var = jnp.mean(jnp.square(o - mu), axis=-1, keepdims=True)
    o = (o - mu) * lax.rsqrt(var + GN_EPS)
    return o.transpose(0, 2, 3, 1, 4).reshape(B, S, H * D).astype(dt)


def moba_attention(q, k, v):
    B, S, H, D = q.shape
    BS = MOBA_BLOCK
    QB = MOBA_QBLOCK
    NB = -(-S // BS)
    NQ = S // QB
    TOPK = min(MOBA_TOPK, NB)
    pad = NB * BS - S
    dt = q.dtype
    slopes = 2.0 ** (-8.0 * jnp.arange(1, H + 1, dtype=F32) / H)

    def to_bh(t):
        return t.astype(F32).transpose(0, 2, 1, 3)

    kb = jnp.pad(to_bh(k), ((0, 0), (0, 0), (0, pad), (0, 0))).reshape(B, H, NB, BS, D)
    vb = jnp.pad(to_bh(v), ((0, 0), (0, 0), (0, pad), (0, 0))).reshape(B, H, NB, BS, D)
    kmean = jnp.mean(kb, axis=3)
    q_blocks = (to_bh(q) * (D ** -0.5)).reshape(B, H, NQ, QB, D).transpose(2, 0, 1, 3, 4)
    offs = jnp.arange(BS, dtype=jnp.int32)
    blk_ids = jnp.arange(NB, dtype=jnp.int32)

    def one_block(args):
        q_blk, qi = args
        ob = (qi * QB) // BS
        t = qi * QB + jnp.arange(QB, dtype=jnp.int32)
        bscore = jnp.einsum('bhqd,bhjd->bhqj', q_blk, kmean)
        bscore = jnp.where(blk_ids < ob, bscore, -jnp.inf)
        _, idx = lax.top_k(bscore, TOPK)
        valid = idx < ob
        gidx = idx.reshape(B, H, QB * TOPK)[:, :, :, None, None]
        kg = jnp.take_along_axis(kb, gidx, axis=2).reshape(B, H, QB, TOPK, BS, D)
        vg = jnp.take_along_axis(vb, gidx, axis=2).reshape(B, H, QB, TOPK, BS, D)
        s_pos = idx[..., None] * BS + offs
        dist_sel = (t[:, None, None] - s_pos).astype(F32)
        s_sel = jnp.einsum('bhqd,bhqjsd->bhqjs', q_blk, kg) - slopes[:, None, None, None] * dist_sel
        s_sel = jnp.where(valid[..., None], s_sel, -jnp.inf)
        k_own = lax.dynamic_index_in_dim(kb, ob, axis=2, keepdims=False)
        v_own = lax.dynamic_index_in_dim(vb, ob, axis=2, keepdims=False)
        own_pos = ob * BS + offs
        dist_own = (t[:, None] - own_pos[None, :]).astype(F32)
        s_own = jnp.einsum('bhqd,bhsd->bhqs', q_blk, k_own) - slopes[:, None, None] * dist_own
        s_own = jnp.where(own_pos[None, :] <= t[:, None], s_own, -jnp.inf)
        scores = jnp.concatenate([s_sel.reshape(B, H, QB, TOPK * BS), s_own], axis=-1)
        p = jax.nn.softmax(scores, axis=-1)
        p_sel = p[..., :TOPK * BS].reshape(B, H, QB, TOPK, BS)
        p_own = p[..., TOPK * BS:]
        return (jnp.einsum('bhqjs,bhqjsd->bhqd', p_sel, vg)
                + jnp.einsum('bhqs,bhsd->bhqd', p_own, v_own))

    out = lax.map(one_block, (q_blocks, jnp.arange(NQ, dtype=jnp.int32)))
    return out.transpose(1, 0, 3, 2, 4).reshape(B, S, H * D).astype(dt)


def gmlp_spatial_gate(u, v, w_s, b_s):
    B, S, W = u.shape
    C = GMLP_CHUNK
    N = S // C
    G = GMLP_GROUPS
    Dg = W // G
    u = jax.nn.gelu(u)
    vf = jax.nn.gelu(v).astype(F32)
    mu = jnp.mean(vf, axis=-1, keepdims=True)
    var = jnp.mean(jnp.square(vf - mu), axis=-1, keepdims=True)
    vn = ((vf - mu) * lax.rsqrt(var + GN_EPS)).reshape(B, N, C, G, Dg)
    w = w_s.astype(F32) * jnp.tril(jnp.ones((C, C), F32))[None]
    mixed = jnp.einsum('gij,bnjgd->bnigd', w, vn) + b_s.astype(F32).T[None, None, :, :, None]
    return u * mixed.reshape(B, S, W).astype(u.dtype)


def mixer_block(x, norm_g, w_in, w_out, gmlp_w, gmlp_b):
    B, S, _ = x.shape
    h = rmsnorm(x, norm_g)
    proj = h @ w_in
    cuts = np.cumsum([RET_WIDTH] * 4 + [MOBA_WIDTH] * 3 + [GMLP_WIDTH] * 2)[:-1].tolist()
    rq, rk, rv, rg, mq, mk, mv, gu, gv = jnp.split(proj, cuts, axis=-1)
    heads = lambda t, n: t.reshape(B, S, n, HEAD_DIM)
    ret = retention(heads(rq, RET_HEADS), heads(rk, RET_HEADS), heads(rv, RET_HEADS)) * jax.nn.silu(rg)
    moba = moba_attention(heads(mq, MOBA_HEADS), heads(mk, MOBA_HEADS), heads(mv, MOBA_HEADS))
    gm = gmlp_spatial_gate(gu, gv, gmlp_w, gmlp_b)
    y = jnp.concatenate([ret, moba, gm], axis=-1) @ w_out
    return x + y


def swiglu(h, w_gate, w_up, w_down):
    return (jax.nn.silu(h @ w_gate) * (h @ w_up)) @ w_down


def moe_swiglu(h, router, we_gate, we_up, we_down):
    logits = (h @ router).astype(F32)
    top_val, top_idx = lax.top_k(logits, TOP_K_EXPERTS)
    gates = jax.nn.softmax(top_val, axis=-1)
    dense_gate = jnp.sum(jax.nn.one_hot(top_idx, N_EXPERTS, dtype=F32) * gates[..., None], axis=-2)
    out = jnp.zeros_like(h)
    for e in range(N_EXPERTS):
        out = out + dense_gate[..., e:e + 1].astype(h.dtype) * swiglu(h, we_gate[e], we_up[e], we_down[e])
    return out


def setup_inputs(seed: int = 0) -> dict:
    key = jax.random.key(seed)
    keys = jax.random.split(key, 24)

    def nrm(i, shape, scale):
        return jax.random.normal(keys[i], shape, F32) * scale

    d = D_MODEL ** -0.5
    f = D_FF ** -0.5
    c = 0.5 * GMLP_CHUNK ** -0.5
    return {
        'x': nrm(0, (BATCH, SEQ, D_MODEL), 1.0),
        'l0_mix_norm': 1.0 + nrm(1, (D_MODEL,), 0.05),
        'l0_w_in': nrm(2, (D_MODEL, IN_WIDTH), d),
        'l0_w_out': nrm(3, (MIX_WIDTH, D_MODEL), MIX_WIDTH ** -0.5),
        'l0_gmlp_w': nrm(4, (GMLP_GROUPS, GMLP_CHUNK, GMLP_CHUNK), c),
        'l0_gmlp_b': 1.0 + nrm(5, (GMLP_GROUPS, GMLP_CHUNK), 0.1),
        'l0_ffn_norm': 1.0 + nrm(6, (D_MODEL,), 0.05),
        'l0_w_gate': nrm(7, (D_MODEL, D_FF), d),
        'l0_w_up': nrm(8, (D_MODEL, D_FF), d),
        'l0_w_down': nrm(9, (D_FF, D_MODEL), f),
        'l1_mix_norm': 1.0 + nrm(10, (D_MODEL,), 0.05),
        'l1_w_in': nrm(11, (D_MODEL, IN_WIDTH), d),
        'l1_w_out': nrm(12, (MIX_WIDTH, D_MODEL), MIX_WIDTH ** -0.5),
        'l1_gmlp_w': nrm(13, (GMLP_GROUPS, GMLP_CHUNK, GMLP_CHUNK), c),
        'l1_gmlp_b': 1.0 + nrm(14, (GMLP_GROUPS, GMLP_CHUNK), 0.1),
        'l1_ffn_norm': 1.0 + nrm(15, (D_MODEL,), 0.05),
        'l1_router': nrm(16, (D_MODEL, N_EXPERTS), d),
        'l1_we_gate': nrm(17, (N_EXPERTS, D_MODEL, D_FF), d),
        'l1_we_up': nrm(18, (N_EXPERTS, D_MODEL, D_FF), d),
        'l1_we_down': nrm(19, (N_EXPERTS, D_FF, D_MODEL), f),
        'final_norm': 1.0 + nrm(20, (D_MODEL,), 0.05),
    }


def reference(x, l0_mix_norm, l0_w_in, l0_w_out, l0_gmlp_w, l0_gmlp_b, l0_ffn_norm,
              l0_w_gate, l0_w_up, l0_w_down,
              l1_mix_norm, l1_w_in, l1_w_out, l1_gmlp_w, l1_gmlp_b, l1_ffn_norm,
              l1_router, l1_we_gate, l1_we_up, l1_we_down, final_norm):
    mix_params = [(l0_mix_norm, l0_w_in, l0_w_out, l0_gmlp_w, l0_gmlp_b),
                  (l1_mix_norm, l1_w_in, l1_w_out, l1_gmlp_w, l1_gmlp_b)]
    ffn_norms = [l0_ffn_norm, l1_ffn_norm]
    ffn_params = [(l0_w_gate, l0_w_up, l0_w_down),
                  (l1_router, l1_we_gate, l1_we_up, l1_we_down)]
    for layer in range(DEPTH):
        x = mixer_block(x, *mix_params[layer])
        h = rmsnorm(x, ffn_norms[layer])
        if layer % 2 == 0:
            x = x + swiglu(h, *ffn_params[layer])
        else:
            x = x + moe_swiglu(h, *ffn_params[layer])
    return rmsnorm(x, final_norm)
```

```python
import functools
import math

import jax
import jax.numpy as jnp
import numpy as np
from jax import lax
from jax.experimental import pallas as pl
from jax.experimental.pallas import tpu as pltpu

F32 = jnp.float32
BF16 = jnp.bfloat16

HEAD_DIM = 64
RET_HEADS = 6
MOBA_HEADS = 6
GMLP_GROUPS = 4
RET_WIDTH = RET_HEADS * HEAD_DIM
MOBA_WIDTH = MOBA_HEADS * HEAD_DIM
GMLP_WIDTH = GMLP_GROUPS * HEAD_DIM
RET_CHUNK = 128
MOBA_BLOCK = 256
MOBA_TOPK = 3
GMLP_CHUNK = 128
N_EXPERTS = 8
NORM_EPS = 1e-6
GN_EPS = 1e-5

LANES = 128
HEAD_PAIRS = RET_HEADS // 2
VMEM_LIMIT = 56 * 1024 * 1024

COL_GU, COL_GV = 0, 2
COL_RQ, COL_RK, COL_RV, COL_RG = 4, 7, 10, 13
COL_MQ, COL_MK, COL_MV = 16, 19, 22
IN_WIDTH = 25 * LANES

NEG_INF = float("-inf")


def _params(*sem):
    return pltpu.CompilerParams(dimension_semantics=sem, vmem_limit_bytes=VMEM_LIMIT)


def _dot(a, b):
    return jnp.dot(a, b, preferred_element_type=F32)


def _dot_nt(a, b):
    return lax.dot_general(a, b, (((1,), (1,)), ((), ())), preferred_element_type=F32)


def _dot_tn(a, b):
    return lax.dot_general(a, b, (((0,), (0,)), ((), ())), preferred_element_type=F32)


def _rmsnorm(x, g):
    ms = jnp.mean(x * x, axis=-1, keepdims=True)
    return (x * lax.rsqrt(ms + NORM_EPS)) * g


def _silu(x):
    return x / (1.0 + jnp.exp(-x))


def _low_half(shape):
    return lax.broadcasted_iota(jnp.int32, shape, len(shape) - 1) < HEAD_DIM


def _pair_mean(x, low):
    s_lo = jnp.sum(jnp.where(low, x, 0.0), axis=-1, keepdims=True)
    s_hi = jnp.sum(jnp.where(low, 0.0, x), axis=-1, keepdims=True)
    return jnp.where(low, s_lo, s_hi) * (1.0 / HEAD_DIM)


def _proj_kernel(x_ref, g_ref, w_ref, o_ref):
    h = _rmsnorm(x_ref[...], g_ref[...]).astype(BF16)
    o_ref[...] = _dot(h, w_ref[...]).astype(o_ref.dtype)


def _proj(x2, g, w_in_bf16, tm=512):
    T, D = x2.shape
    N = w_in_bf16.shape[1]
    return pl.pallas_call(
        _proj_kernel,
        grid=(T // tm,),
        in_specs=[pl.BlockSpec((tm, D), lambda i: (i, 0)),
                  pl.BlockSpec((1, D), lambda i: (0, 0)),
                  pl.BlockSpec((D, N), lambda i: (0, 0))],
        out_specs=pl.BlockSpec((tm, N), lambda i: (i, 0)),
        out_shape=jax.ShapeDtypeStruct((T, N), BF16),
        compiler_params=_params("parallel"),
        name="proj",
    )(x2, g.reshape(1, D), w_in_bf16)


def _retention_tables():
    C, H, D = RET_CHUNK, RET_HEADS, HEAD_DIM
    log_g = np.log(1.0 - 2.0 ** (-5.0 - np.arange(H, dtype=np.float64)))
    pos = np.arange(C, dtype=np.float64)
    diff = pos[:, None] - pos[None, :]
    decay = np.where(diff >= 0, np.exp(log_g[:, None, None] * np.maximum(diff, 0.0)), 0.0)
    decay = decay * D ** -0.5
    zeta = np.exp(log_g[:, None] * (C - 1 - pos)) * D ** -0.5
    xi = np.exp(log_g[:, None] * (pos + 1))
    cdec = np.exp(log_g * C)

    def lanes(t):
        return np.repeat(t.reshape(HEAD_PAIRS, 2, C).transpose(0, 2, 1), D, axis=2)

    cd = np.repeat(cdec.reshape(HEAD_PAIRS, 1, 2), D, axis=2)
    f = lambda a: jnp.asarray(a, F32)
    return f(decay.reshape(HEAD_PAIRS, 2, C, C)), f(lanes(zeta)), f(lanes(xi)), f(cd)


def _retention_kernel(q_ref, k_ref, v_ref, g_ref, dec_ref, zeta_ref, xi_ref, cd_ref, o_ref,
                      state_ref, *, chunks):
    C = RET_CHUNK

    @pl.when(pl.program_id(2) == 0)
    def _():
        state_ref[...] = jnp.zeros_like(state_ref)

    low = _low_half((C, LANES))
    r = lax.broadcasted_iota(jnp.int32, (LANES, LANES), 0) < HEAD_DIM
    c = lax.broadcasted_iota(jnp.int32, (LANES, LANES), 1) < HEAD_DIM
    same_head = r == c
    zeta = zeta_ref[0]
    xi = xi_ref[0]
    cd = cd_ref[0]
    zero = jnp.zeros((), BF16)

    for ci in range(chunks):
        rows = pl.ds(ci * C, C)
        q = q_ref[rows, :]
        k = k_ref[rows, :]
        v = v_ref[rows, :]
        state = state_ref[...]
        p0 = (_dot_nt(jnp.where(low, q, zero), k) * dec_ref[0, 0]).astype(BF16)
        p1 = (_dot_nt(jnp.where(low, zero, q), k) * dec_ref[0, 1]).astype(BF16)
        intra = _dot(p0, jnp.where(low, v, zero)) + _dot(p1, jnp.where(low, zero, v))
        cross = _dot((q.astype(F32) * xi).astype(BF16), state.astype(BF16))
        o = intra + cross
        kz = (k.astype(F32) * zeta).astype(BF16)
        kv = _dot_tn(kz, v)
        state_ref[...] = state * cd + jnp.where(same_head, kv, 0.0)
        mu = _pair_mean(o, low)
        d = o - mu
        var = _pair_mean(d * d, low)
        on = d * lax.rsqrt(var + GN_EPS)
        o_ref[rows, :] = (on * _silu(g_ref[rows, :].astype(F32))).astype(o_ref.dtype)


def _retention(proj, B, S, ts=512):
    T = B * S
    nst = S // ts
    dec, zeta, xi, cd = _retention_tables()
    C = RET_CHUNK

    def col(c0):
        return pl.BlockSpec((ts, LANES), lambda b, p, s: (b * nst + s, c0 + p))

    return pl.pallas_call(
        functools.partial(_retention_kernel, chunks=ts // C),
        grid=(B, HEAD_PAIRS, nst),
        in_specs=[col(COL_RQ), col(COL_RK), col(COL_RV), col(COL_RG),
                  pl.BlockSpec((1, 2, C, C), lambda b, p, s: (p, 0, 0, 0)),
                  pl.BlockSpec((1, C, LANES), lambda b, p, s: (p, 0, 0)),
                  pl.BlockSpec((1, C, LANES), lambda b, p, s: (p, 0, 0)),
                  pl.BlockSpec((1, 1, LANES), lambda b, p, s: (p, 0, 0))],
        out_specs=pl.BlockSpec((ts, LANES), lambda b, p, s: (b * nst + s, p)),
        out_shape=jax.ShapeDtypeStruct((T, RET_WIDTH), BF16),
        scratch_shapes=[pltpu.VMEM((LANES, LANES), F32)],
        compiler_params=_params("parallel", "parallel", "arbitrary"),
        name="retention",
    )(proj, proj, proj, proj, dec, zeta, xi, cd)


def _moba_kernel(slope_ref, q_ref, k_ref, v_ref, o_ref, kmean_ref, sel_ref, m_ref, l_ref, acc_ref, *, nblocks):
    BS = MOBA_BLOCK
    p = pl.program_id(1)
    qi = pl.program_id(2)

    @pl.when(qi == 0)
    def _():
        kmean_ref[...] = jnp.zeros_like(kmean_ref)
        kf = k_ref[...].astype(F32).reshape(nblocks, BS, LANES)
        kmean_ref[0:nblocks, :] = jnp.mean(kf, axis=1)

    q = q_ref[...]
    low = _low_half((BS, LANES))
    lane = lax.broadcasted_iota(jnp.int32, (BS, LANES), 1)
    row = lax.broadcasted_iota(jnp.int32, (BS, BS), 0)
    colk = lax.broadcasted_iota(jnp.int32, (BS, BS), 1)
    rel = (colk - row).astype(F32)
    causal = colk <= row
    zero = jnp.zeros((), BF16)
    kmean = kmean_ref[...].astype(BF16)
    k_own = k_ref[pl.ds(pl.multiple_of(qi * BS, BS), BS), :]
    v_own = v_ref[pl.ds(pl.multiple_of(qi * BS, BS), BS), :]

    slopes = []
    qms = []
    for hh in range(2):
        slope = slope_ref[2 * p + hh]
        slopes.append(slope)
        qm = jnp.where(low, q, zero) if hh == 0 else jnp.where(low, zero, q)
        qms.append(qm)
        bs = _dot_nt(qm, kmean) * HEAD_DIM ** -0.5
        valid = lane < qi
        sc = jnp.where(valid, bs, NEG_INF)
        sel = jnp.zeros((BS, LANES), F32)
        for _ in range(MOBA_TOPK):
            mx = jnp.max(sc, axis=1, keepdims=True)
            idx = jnp.min(jnp.where(sc == mx, lane, LANES), axis=1, keepdims=True)
            pick = (lane == idx) & (mx > NEG_INF)
            sel = jnp.where(pick, 1.0, sel)
            sc = jnp.where(pick, NEG_INF, sc)
        sel_ref[hh] = sel
        s = _dot_nt(qm, k_own) * HEAD_DIM ** -0.5 + slope * rel
        s = jnp.where(causal, s, NEG_INF)
        m = jnp.max(s, axis=1, keepdims=True)
        e = jnp.exp(s - m)
        m_ref[hh] = m
        l_ref[hh] = jnp.sum(e, axis=1, keepdims=True)
        acc_ref[hh] = _dot(e.astype(BF16), v_own)

    def body(j, carry):
        k_j = k_ref[pl.ds(pl.multiple_of(j * BS, BS), BS), :]
        v_j = v_ref[pl.ds(pl.multiple_of(j * BS, BS), BS), :]
        off = ((j - qi) * BS).astype(F32)
        for hh in range(2):
            chosen = jnp.sum(jnp.where(lane == j, sel_ref[hh], 0.0), axis=1, keepdims=True)
            s = _dot_nt(qms[hh], k_j) * HEAD_DIM ** -0.5 + slopes[hh] * (rel + off)
            s = jnp.where(jnp.broadcast_to(chosen, (BS, BS)) > 0.0, s, NEG_INF)
            m_old = m_ref[hh]
            m_new = jnp.maximum(m_old, jnp.max(s, axis=1, keepdims=True))
            alpha = jnp.exp(m_old - m_new)
            e = jnp.exp(s - m_new)
            m_ref[hh] = m_new
            l_ref[hh] = l_ref[hh] * alpha + jnp.sum(e, axis=1, keepdims=True)
            acc_ref[hh] = acc_ref[hh] * alpha + _dot(e.astype(BF16), v_j)
        return carry

    lax.fori_loop(0, qi, body, 0)
    out0 = acc_ref[0] / l_ref[0]
    out1 = acc_ref[1] / l_ref[1]
    o_ref[...] = jnp.where(low, out0, out1).astype(o_ref.dtype)


def _moba(proj, B, S):
    T = B * S
    BS = MOBA_BLOCK
    nb = S // BS
    slopes = jnp.asarray(2.0 ** (-8.0 * np.arange(1, MOBA_HEADS + 1) / MOBA_HEADS), F32)
    return pl.pallas_call(
        functools.partial(_moba_kernel, nblocks=nb),
        grid=(B, HEAD_PAIRS, nb),
        in_specs=[pl.BlockSpec(memory_space=pltpu.SMEM),
                  pl.BlockSpec((BS, LANES), lambda b, p, i: (b * nb + i, COL_MQ + p)),
                  pl.BlockSpec((S, LANES), lambda b, p, i: (b, COL_MK + p)),
                  pl.BlockSpec((S, LANES), lambda b, p, i: (b, COL_MV + p))],
        out_specs=pl.BlockSpec((BS, LANES), lambda b, p, i: (b * nb + i, p)),
        out_shape=jax.ShapeDtypeStruct((T, MOBA_WIDTH), BF16),
        scratch_shapes=[pltpu.VMEM((LANES, LANES), F32),
                        pltpu.VMEM((2, BS, LANES), F32),
                        pltpu.VMEM((2, BS, 1), F32),
                        pltpu.VMEM((2, BS, 1), F32),
                        pltpu.VMEM((2, BS, LANES), F32)],
        compiler_params=_params("parallel", "parallel", "arbitrary"),
        name="moba",
    )(slopes, proj, proj, proj)


def _gmlp_kernel(u_ref, v_ref, w_ref, b_ref, o_ref, *, chunks):
    C = GMLP_CHUNK
    low = _low_half((C, LANES))
    r = lax.broadcasted_iota(jnp.int32, (C, 2 * C), 0)
    c = lax.broadcasted_iota(jnp.int32, (C, 2 * C), 1)
    tril = jnp.where(c >= C, c - C, c) <= r
    ws = [jnp.where(tril, w_ref[pr], 0.0).astype(BF16) for pr in range(2)]
    for ci in range(chunks):
        rows = pl.ds(ci * C, C)
        u = jax.nn.gelu(u_ref[rows, :].astype(F32))
        vf = jax.nn.gelu(v_ref[rows, :].astype(F32))
        mu = jnp.mean(vf, axis=-1, keepdims=True)
        d = vf - mu
        var = jnp.mean(d * d, axis=-1, keepdims=True)
        vn = d * lax.rsqrt(var + GN_EPS)
        mixed = []
        for pr in range(2):
            vp = vn[:, pr * LANES:(pr + 1) * LANES]
            stacked = jnp.concatenate([jnp.where(low, vp, 0.0), jnp.where(low, 0.0, vp)], axis=0)
            mixed.append(_dot(ws[pr], stacked.astype(BF16)))
        mixed = jnp.concatenate(mixed, axis=1) + b_ref[...]
        o_ref[rows, :] = (u * mixed).astype(o_ref.dtype)


def _gmlp(proj, gmlp_w, gmlp_b, T, ts=512):
    C, G = GMLP_CHUNK, GMLP_GROUPS
    w_cat = gmlp_w.astype(F32).reshape(G // 2, 2, C, C).transpose(0, 2, 1, 3).reshape(G // 2, C, 2 * C)
    b_tab = jnp.repeat(gmlp_b.astype(F32).T, HEAD_DIM, axis=1)
    W = GMLP_WIDTH
    return pl.pallas_call(
        functools.partial(_gmlp_kernel, chunks=ts // C),
        grid=(T // ts,),
        in_specs=[pl.BlockSpec((ts, W), lambda i: (i, COL_GU * LANES // W)),
                  pl.BlockSpec((ts, W), lambda i: (i, COL_GV * LANES // W)),
                  pl.BlockSpec((G // 2, C, 2 * C), lambda i: (0, 0, 0)),
                  pl.BlockSpec((C, W), lambda i: (0, 0))],
        out_specs=pl.BlockSpec((ts, W), lambda i: (i, 0)),
        out_shape=jax.ShapeDtypeStruct((T, W), BF16),
        compiler_params=_params("parallel"),
        name="gmlp",
    )(proj, proj, w_cat, b_tab)


def _outproj_kernel(x_ref, r_ref, m_ref, g_ref, w_ref, o_ref):
    mix = jnp.concatenate([r_ref[...], m_ref[...], g_ref[...]], axis=1)
    o_ref[...] = x_ref[...] + _dot(mix, w_ref[...])


def _outproj(x2, ret, moba, gm, w_out_bf16, tm=512):
    T, D = x2.shape
    row = lambda w: pl.BlockSpec((tm, w), lambda i: (i, 0))
    return pl.pallas_call(
        _outproj_kernel,
        grid=(T // tm,),
        in_specs=[row(D), row(RET_WIDTH), row(MOBA_WIDTH), row(GMLP_WIDTH),
                  pl.BlockSpec(w_out_bf16.shape, lambda i: (0, 0))],
        out_specs=row(D),
        out_shape=jax.ShapeDtypeStruct((T, D), F32),
        compiler_params=_params("parallel"),
        name="outproj",
    )(x2, ret, moba, gm, w_out_bf16)


def _ffn_kernel(x_ref, g_ref, wg_ref, wu_ref, wd_ref, o_ref, h_ref, acc_ref):
    f = pl.program_id(1)

    @pl.when(f == 0)
    def _():
        x = x_ref[...]
        h_ref[...] = _rmsnorm(x, g_ref[...]).astype(BF16)
        acc_ref[...] = x

    h = h_ref[...]
    a = _silu(_dot(h, wg_ref[...])) * _dot(h, wu_ref[...])
    acc_ref[...] += _dot(a.astype(BF16), wd_ref[...])

    @pl.when(f == pl.num_programs(1) - 1)
    def _():
        o_ref[...] = acc_ref[...]


def _ffn(x2, g, wg, wu, wd, tm=1024, tf=512):
    T, D = x2.shape
    F = wg.shape[1]
    return pl.pallas_call(
        _ffn_kernel,
        grid=(T // tm, F // tf),
        in_specs=[pl.BlockSpec((tm, D), lambda i, f: (i, 0)),
                  pl.BlockSpec((1, D), lambda i, f: (0, 0)),
                  pl.BlockSpec((D, tf), lambda i, f: (0, f)),
                  pl.BlockSpec((D, tf), lambda i, f: (0, f)),
                  pl.BlockSpec((tf, D), lambda i, f: (f, 0))],
        out_specs=pl.BlockSpec((tm, D), lambda i, f: (i, 0)),
        out_shape=jax.ShapeDtypeStruct((T, D), F32),
        scratch_shapes=[pltpu.VMEM((tm, D), BF16), pltpu.VMEM((tm, D), F32)],
        compiler_params=_params("parallel", "arbitrary"),
        name="ffn",
    )(x2, g.reshape(1, D), wg, wu, wd)


def _moe_kernel(x_ref, g_ref, r_ref, wg_ref, wu_ref, wd_ref, fg_ref, o_ref,
                h_ref, acc_ref, gates_ref, gcol_ref):
    e = pl.program_id(1)
    f = pl.program_id(2)
    tm = x_ref.shape[0]
    tf = wg_ref.shape[2]
    lane = lax.broadcasted_iota(jnp.int32, (tm, LANES), 1)

    @pl.when((e == 0) & (f == 0))
    def _():
        x = x_ref[...]
        h = _rmsnorm(x, g_ref[...]).astype(BF16)
        h_ref[...] = h
        acc_ref[...] = x
        logits = jnp.where(lane < N_EXPERTS, _dot(h, r_ref[...]), NEG_INF)
        m1 = jnp.max(logits, axis=1, keepdims=True)
        i1 = jnp.min(jnp.where(logits == m1, lane, LANES), axis=1, keepdims=True)
        rest = jnp.where(lane == i1, NEG_INF, logits)
        m2 = jnp.max(rest, axis=1, keepdims=True)
        i2 = jnp.min(jnp.where(rest == m2, lane, LANES), axis=1, keepdims=True)
        e2 = jnp.exp(m2 - m1)
        g1 = 1.0 / (1.0 + e2)
        g2 = e2 / (1.0 + e2)
        gates_ref[...] = jnp.where(lane == i1, g1, 0.0) + jnp.where(lane == i2, g2, 0.0)

    @pl.when(f == 0)
    def _():
        col = jnp.sum(jnp.where(lane == e, gates_ref[...], 0.0), axis=1, keepdims=True)
        gcol_ref[...] = jnp.broadcast_to(col, (tm, LANES))

    h = h_ref[...]
    a = _silu(_dot(h, wg_ref[0])) * _dot(h, wu_ref[0])
    a = a * jnp.concatenate([gcol_ref[...]] * (tf // LANES), axis=1)
    acc_ref[...] += _dot(a.astype(BF16), wd_ref[0])

    @pl.when((e == pl.num_programs(1) - 1) & (f == pl.num_programs(2) - 1))
    def _():
        o_ref[...] = _rmsnorm(acc_ref[...], fg_ref[...])


def _moe(x2, g, router, wg, wu, wd, final_g, tm=1024, tf=512):
    T, D = x2.shape
    E, _, F = wg.shape
    r_pad = jnp.zeros((D, LANES), BF16).at[:, :E].set(router.astype(BF16))
    return pl.pallas_call(
        _moe_kernel,
        grid=(T // tm, E, F // tf),
        in_specs=[pl.BlockSpec((tm, D), lambda i, e, f: (i, 0)),
                  pl.BlockSpec((1, D), lambda i, e, f: (0, 0)),
                  pl.BlockSpec((D, LANES), lambda i, e, f: (0, 0)),
                  pl.BlockSpec((1, D, tf), lambda i, e, f: (e, 0, f)),
                  pl.BlockSpec((1, D, tf), lambda i, e, f: (e, 0, f)),
                  pl.BlockSpec((1, tf, D), lambda i, e, f: (e, f, 0)),
                  pl.BlockSpec((1, D), lambda i, e, f: (0, 0))],
        out_specs=pl.BlockSpec((tm, D), lambda i, e, f: (i, 0)),
        out_shape=jax.ShapeDtypeStruct((T, D), F32),
        scratch_shapes=[pltpu.VMEM((tm, D), BF16), pltpu.VMEM((tm, D), F32),
                        pltpu.VMEM((tm, LANES), F32), pltpu.VMEM((tm, LANES), F32)],
        compiler_params=_params("parallel", "arbitrary", "arbitrary"),
        name="moe",
    )(x2, g.reshape(1, D), r_pad, wg, wu, wd, final_g.reshape(1, D))


def _permute_w_in(w_in):
    n_ret, n_moba = 4 * RET_WIDTH, 3 * MOBA_WIDTH
    gm = w_in[:, n_ret + n_moba:]
    return jnp.concatenate([gm, w_in[:, :n_ret + n_moba]], axis=1).astype(BF16)


def _mixer(x2, B, S, norm_g, w_in, w_out, gmlp_w, gmlp_b):
    T = B * S
    proj = _proj(x2, norm_g, _permute_w_in(w_in))
    ret = _retention(proj, B, S)
    moba = _moba(proj, B, S)
    gm = _gmlp(proj, gmlp_w, gmlp_b, T)
    return _outproj(x2, ret, moba, gm, w_out.astype(BF16))


def kernel(x, l0_mix_norm, l0_w_in, l0_w_out, l0_gmlp_w, l0_gmlp_b, l0_ffn_norm, l0_w_gate, l0_w_up, l0_w_down, l1_mix_norm, l1_w_in, l1_w_out, l1_gmlp_w, l1_gmlp_b, l1_ffn_norm, l1_router, l1_we_gate, l1_we_up, l1_we_down, final_norm):
    B, S, D = x.shape
    x2 = x.reshape(B * S, D)
    x2 = _mixer(x2, B, S, l0_mix_norm, l0_w_in, l0_w_out, l0_gmlp_w, l0_gmlp_b)
    x2 = _ffn(x2, l0_ffn_norm, l0_w_gate.astype(BF16), l0_w_up.astype(BF16), l0_w_down.astype(BF16))
    x2 = _mixer(x2, B, S, l1_mix_norm, l1_w_in, l1_w_out, l1_gmlp_w, l1_gmlp_b)
    x2 = _moe(x2, l1_ffn_norm, l1_router, l1_we_gate.astype(BF16), l1_we_up.astype(BF16),
              l1_we_down.astype(BF16), final_norm)
    return x2.reshape(B, S, D)
```

```python
import functools
import math

import jax
import jax.numpy as jnp
import numpy as np
from jax import lax
from jax.experimental import pallas as pl
from jax.experimental.pallas import tpu as pltpu

F32 = jnp.float32
BF16 = jnp.bfloat16

HEAD_DIM = 64
RET_HEADS = 6
MOBA_HEADS = 6
GMLP_GROUPS = 4
RET_WIDTH = RET_HEADS * HEAD_DIM
MOBA_WIDTH = MOBA_HEADS * HEAD_DIM
GMLP_WIDTH = GMLP_GROUPS * HEAD_DIM
RET_CHUNK = 128
MOBA_BLOCK = 256
MOBA_TOPK = 3
GMLP_CHUNK = 128
N_EXPERTS = 8
NORM_EPS = 1e-6
GN_EPS = 1e-5

LANES = 128
HEAD_PAIRS = RET_HEADS // 2
VMEM_LIMIT = 56 * 1024 * 1024
FFN_TOKENS = 2048
FFN_COLS = 512

COL_GU, COL_GV = 0, 2
COL_RQ, COL_RK, COL_RV, COL_RG = 4, 7, 10, 13
COL_MQ, COL_MK, COL_MV = 16, 19, 22
IN_WIDTH = 25 * LANES

NEG_INF = float("-inf")


def _params(*sem):
    return pltpu.CompilerParams(dimension_semantics=sem, vmem_limit_bytes=VMEM_LIMIT)


def _dot(a, b):
    return jnp.dot(a, b, preferred_element_type=F32)


def _dot_nt(a, b):
    return lax.dot_general(a, b, (((1,), (1,)), ((), ())), preferred_element_type=F32)


def _dot_tn(a, b):
    return lax.dot_general(a, b, (((0,), (0,)), ((), ())), preferred_element_type=F32)


def _rmsnorm(x, g):
    ms = jnp.mean(x * x, axis=-1, keepdims=True)
    return (x * lax.rsqrt(ms + NORM_EPS)) * g


def _silu(x):
    return x / (1.0 + jnp.exp(-x))


def _low_half(shape):
    return lax.broadcasted_iota(jnp.int32, shape, len(shape) - 1) < HEAD_DIM


def _pair_mean(x, low):
    s_lo = jnp.sum(jnp.where(low, x, 0.0), axis=-1, keepdims=True)
    s_hi = jnp.sum(jnp.where(low, 0.0, x), axis=-1, keepdims=True)
    return jnp.where(low, s_lo, s_hi) * (1.0 / HEAD_DIM)


def _proj_kernel(x_ref, g_ref, w_ref, o_ref):
    h = _rmsnorm(x_ref[...], g_ref[...]).astype(BF16)
    o_ref[...] = _dot(h, w_ref[...]).astype(o_ref.dtype)


def _proj(x2, g, w_in_bf16, tm=512):
    T, D = x2.shape
    N = w_in_bf16.shape[1]
    return pl.pallas_call(
        _proj_kernel,
        grid=(T // tm,),
        in_specs=[pl.BlockSpec((tm, D), lambda i: (i, 0)),
                  pl.BlockSpec((1, D), lambda i: (0, 0)),
                  pl.BlockSpec((D, N), lambda i: (0, 0))],
        out_specs=pl.BlockSpec((tm, N), lambda i: (i, 0)),
        out_shape=jax.ShapeDtypeStruct((T, N), BF16),
        compiler_params=_params("parallel"),
        name="proj",
    )(x2, g.reshape(1, D), w_in_bf16)


def _retention_tables():
    C, H, D = RET_CHUNK, RET_HEADS, HEAD_DIM
    log_g = np.log(1.0 - 2.0 ** (-5.0 - np.arange(H, dtype=np.float64)))
    pos = np.arange(C, dtype=np.float64)
    diff = pos[:, None] - pos[None, :]
    decay = np.where(diff >= 0, np.exp(log_g[:, None, None] * np.maximum(diff, 0.0)), 0.0)
    decay = decay * D ** -0.5
    zeta = np.exp(log_g[:, None] * (C - 1 - pos)) * D ** -0.5
    xi = np.exp(log_g[:, None] * (pos + 1))
    cdec = np.exp(log_g * C)

    def lanes(t):
        return np.repeat(t.reshape(HEAD_PAIRS, 2, C).transpose(0, 2, 1), D, axis=2)

    cd = np.repeat(cdec.reshape(HEAD_PAIRS, 1, 2), D, axis=2)
    f = lambda a: jnp.asarray(a, F32)
    return f(decay.reshape(HEAD_PAIRS, 2, C, C)), f(lanes(zeta)), f(lanes(xi)), f(cd)


def _retention_kernel(q_ref, k_ref, v_ref, g_ref, dec_ref, zeta_ref, xi_ref, cd_ref, o_ref,
                      state_ref, *, chunks):
    C = RET_CHUNK

    @pl.when(pl.program_id(2) == 0)
    def _():
        state_ref[...] = jnp.zeros_like(state_ref)

    low = _low_half((C, LANES))
    r = lax.broadcasted_iota(jnp.int32, (LANES, LANES), 0) < HEAD_DIM
    c = lax.broadcasted_iota(jnp.int32, (LANES, LANES), 1) < HEAD_DIM
    same_head = r == c
    zeta = zeta_ref[0]
    xi = xi_ref[0]
    cd = cd_ref[0]
    zero = jnp.zeros((), BF16)

    for ci in range(chunks):
        rows = pl.ds(ci * C, C)
        q = q_ref[rows, :]
        k = k_ref[rows, :]
        v = v_ref[rows, :]
        state = state_ref[...]
        p0 = (_dot_nt(jnp.where(low, q, zero), k) * dec_ref[0, 0]).astype(BF16)
        p1 = (_dot_nt(jnp.where(low, zero, q), k) * dec_ref[0, 1]).astype(BF16)
        intra = _dot(p0, jnp.where(low, v, zero)) + _dot(p1, jnp.where(low, zero, v))
        cross = _dot((q.astype(F32) * xi).astype(BF16), state.astype(BF16))
        o = intra + cross
        kz = (k.astype(F32) * zeta).astype(BF16)
        kv = _dot_tn(kz, v)
        state_ref[...] = state * cd + jnp.where(same_head, kv, 0.0)
        mu = _pair_mean(o, low)
        d = o - mu
        var = _pair_mean(d * d, low)
        on = d * lax.rsqrt(var + GN_EPS)
        o_ref[rows, :] = (on * _silu(g_ref[rows, :].astype(F32))).astype(o_ref.dtype)


def _retention(proj, B, S, ts=512):
    T = B * S
    nst = S // ts
    dec, zeta, xi, cd = _retention_tables()
    C = RET_CHUNK

    def col(c0):
        return pl.BlockSpec((ts, LANES), lambda b, p, s: (b * nst + s, c0 + p))

    return pl.pallas_call(
        functools.partial(_retention_kernel, chunks=ts // C),
        grid=(B, HEAD_PAIRS, nst),
        in_specs=[col(COL_RQ), col(COL_RK), col(COL_RV), col(COL_RG),
                  pl.BlockSpec((1, 2, C, C), lambda b, p, s: (p, 0, 0, 0)),
                  pl.BlockSpec((1, C, LANES), lambda b, p, s: (p, 0, 0)),
                  pl.BlockSpec((1, C, LANES), lambda b, p, s: (p, 0, 0)),
                  pl.BlockSpec((1, 1, LANES), lambda b, p, s: (p, 0, 0))],
        out_specs=pl.BlockSpec((ts, LANES), lambda b, p, s: (b * nst + s, p)),
        out_shape=jax.ShapeDtypeStruct((T, RET_WIDTH), BF16),
        scratch_shapes=[pltpu.VMEM((LANES, LANES), F32)],
        compiler_params=_params("parallel", "parallel", "arbitrary"),
        name="retention",
    )(proj, proj, proj, proj, dec, zeta, xi, cd)


AUG_SLOPE_ROWS = 6
AUG_BIAS_ROW0 = 16
MASK_BIAS = -1e30


def _moba_kernel(slope_ref, q_ref, k_ref, v_ref, o_ref,
                 ka_ref, vt_ref, kmean_ref, qa_ref, m_ref, l_ref, acc_ref, *, nblocks):
    BS = MOBA_BLOCK
    p = pl.program_id(1)
    qi = pl.program_id(2)

    @pl.when(qi == 0)
    def _():
        lane = lax.broadcasted_iota(jnp.int32, (BS, LANES), 1)
        low = lane < HEAD_DIM
        offs = lax.broadcasted_iota(jnp.int32, (BS, LANES), 0).astype(F32)
        kmean_ref[...] = jnp.zeros_like(kmean_ref)
        for jb in range(nblocks):
            kb = k_ref[jb * BS:(jb + 1) * BS, :]
            for hh in range(2):
                a = lane - HEAD_DIM if hh == 0 else lane
                aug = jnp.where(a < 3, offs,
                                jnp.where(a < AUG_SLOPE_ROWS, float(jb * BS),
                                          jnp.where(a == AUG_BIAS_ROW0 + jb, 1.0, 0.0))).astype(BF16)
                ka_ref[hh, jb] = jnp.where(low, kb, aug) if hh == 0 else jnp.where(low, aug, kb)
            kmean_ref[jb:jb + 1, :] = jnp.mean(kb.astype(F32), axis=0, keepdims=True)
            vt_ref[jb] = v_ref[jb * BS:(jb + 1) * BS, :].astype(F32).T.astype(BF16)

    q_t = (q_ref[...].astype(F32) * HEAD_DIM ** -0.5).T
    kmean = kmean_ref[...].astype(BF16)
    r16 = lax.broadcasted_iota(jnp.int32, (AUG_BIAS_ROW0, BS), 0)
    key_off = lax.broadcasted_iota(jnp.int32, (BS, BS), 0)
    qry_off = lax.broadcasted_iota(jnp.int32, (BS, BS), 1)
    causal = key_off <= qry_off
    zeros_h = jnp.zeros((HEAD_DIM, BS), F32)
    pad = jnp.zeros((HEAD_DIM - 2 * AUG_BIAS_ROW0, BS), F32)

    for hh in range(2):
        qh = q_t[hh * HEAD_DIM:(hh + 1) * HEAD_DIM, :]
        qm_t = jnp.concatenate([qh, zeros_h] if hh == 0 else [zeros_h, qh], axis=0).astype(BF16)
        bs = _dot(kmean, qm_t)[0:AUG_BIAS_ROW0, :]
        sc = jnp.where(r16 < qi, bs, NEG_INF)
        bias = jnp.full((AUG_BIAS_ROW0, BS), MASK_BIAS, F32)
        for _ in range(MOBA_TOPK):
            mx = jnp.max(sc, axis=0, keepdims=True)
            idx = jnp.min(jnp.where(sc == mx, r16, AUG_BIAS_ROW0), axis=0, keepdims=True)
            pick = (r16 == idx) & (mx > NEG_INF)
            bias = jnp.where(pick, 0.0, bias)
            sc = jnp.where(pick, NEG_INF, sc)
        base = 3 * (2 * p + hh)
        s1, s2, s3 = slope_ref[base], slope_ref[base + 1], slope_ref[base + 2]
        piece = r16 % 3
        srow = jnp.where(r16 < AUG_SLOPE_ROWS,
                         jnp.where(piece == 0, s1, jnp.where(piece == 1, s2, s3)), 0.0)

        def with_aug(bias_rows):
            aug = jnp.concatenate([srow, bias_rows, pad], axis=0)
            return jnp.concatenate([qh, aug] if hh == 0 else [aug, qh], axis=0).astype(BF16)

        qa_ref[hh] = with_aug(bias)
        qa_own = with_aug(jnp.where(r16 == qi, 0.0, bias))
        s = jnp.where(causal, _dot(ka_ref[hh, qi], qa_own), NEG_INF)
        m = jnp.max(s, axis=0, keepdims=True)
        e = jnp.exp(s - m)
        m_ref[hh] = m
        l_ref[hh] = jnp.sum(e, axis=0, keepdims=True)
        acc_ref[hh] = _dot(vt_ref[qi, hh * HEAD_DIM:(hh + 1) * HEAD_DIM, :], e.astype(BF16))

    def body(jj, carry):
        j0 = 2 * jj
        ss = []
        for hh in range(2):
            ka = jnp.concatenate([ka_ref[hh, j0], ka_ref[hh, j0 + 1]], axis=0)
            ss.append(_dot(ka, qa_ref[hh]))
        for hh in range(2):
            s = ss[hh]
            m_old = m_ref[hh]
            m_new = jnp.maximum(m_old, jnp.max(s, axis=0, keepdims=True))
            alpha = jnp.exp(m_old - m_new)
            e = jnp.exp(s - m_new)
            m_ref[hh] = m_new
            l_ref[hh] = l_ref[hh] * alpha + jnp.sum(e, axis=0, keepdims=True)
            rows = slice(hh * HEAD_DIM, (hh + 1) * HEAD_DIM)
            vt = jnp.concatenate([vt_ref[j0, rows, :], vt_ref[j0 + 1, rows, :]], axis=1)
            acc_ref[hh] = acc_ref[hh] * alpha + _dot(vt, e.astype(BF16))
        return carry

    lax.fori_loop(0, (qi + 1) // 2, body, 0)
    out_t = jnp.concatenate([acc_ref[0] / l_ref[0], acc_ref[1] / l_ref[1]], axis=0)
    o_ref[...] = out_t.T.astype(o_ref.dtype)


def _moba(proj, B, S):
    T = B * S
    BS = MOBA_BLOCK
    nb = S // BS
    slopes = jnp.asarray(2.0 ** (-8.0 * np.arange(1, MOBA_HEADS + 1) / MOBA_HEADS), F32)
    s1 = slopes.astype(BF16).astype(F32)
    s2 = (slopes - s1).astype(BF16).astype(F32)
    s3 = (slopes - s1 - s2).astype(BF16).astype(F32)
    slopes = jnp.stack([s1, s2, s3], axis=1).reshape(-1)
    return pl.pallas_call(
        functools.partial(_moba_kernel, nblocks=nb),
        grid=(B, HEAD_PAIRS, nb),
        in_specs=[pl.BlockSpec(memory_space=pltpu.SMEM),
                  pl.BlockSpec((BS, LANES), lambda b, p, i: (b * nb + i, COL_MQ + p)),
                  pl.BlockSpec((S, LANES), lambda b, p, i: (b, COL_MK + p)),
                  pl.BlockSpec((S, LANES), lambda b, p, i: (b, COL_MV + p))],
        out_specs=pl.BlockSpec((BS, LANES), lambda b, p, i: (b * nb + i, p)),
        out_shape=jax.ShapeDtypeStruct((T, MOBA_WIDTH), BF16),
        scratch_shapes=[pltpu.VMEM((2, nb, BS, LANES), BF16),
                        pltpu.VMEM((nb, LANES, BS), BF16),
                        pltpu.VMEM((LANES, LANES), F32),
                        pltpu.VMEM((2, LANES, BS), BF16),
                        pltpu.VMEM((2, 1, BS), F32),
                        pltpu.VMEM((2, 1, BS), F32),
                        pltpu.VMEM((2, HEAD_DIM, BS), F32)],
        compiler_params=_params("parallel", "parallel", "arbitrary"),
        name="moba",
    )(slopes, proj, proj, proj)


def _gmlp_kernel(u_ref, v_ref, w_ref, b_ref, o_ref, *, chunks):
    C = GMLP_CHUNK
    low = _low_half((C, LANES))
    r = lax.broadcasted_iota(jnp.int32, (C, 2 * C), 0)
    c = lax.broadcasted_iota(jnp.int32, (C, 2 * C), 1)
    tril = jnp.where(c >= C, c - C, c) <= r
    ws = [jnp.where(tril, w_ref[pr], 0.0).astype(BF16) for pr in range(2)]
    for ci in range(chunks):
        rows = pl.ds(ci * C, C)
        u = jax.nn.gelu(u_ref[rows, :].astype(F32))
        vf = jax.nn.gelu(v_ref[rows, :].astype(F32))
        mu = jnp.mean(vf, axis=-1, keepdims=True)
        d = vf - mu
        var = jnp.mean(d * d, axis=-1, keepdims=True)
        vn = d * lax.rsqrt(var + GN_EPS)
        mixed = []
        for pr in range(2):
            vp = vn[:, pr * LANES:(pr + 1) * LANES]
            stacked = jnp.concatenate([jnp.where(low, vp, 0.0), jnp.where(low, 0.0, vp)], axis=0)
            mixed.append(_dot(ws[pr], stacked.astype(BF16)))
        mixed = jnp.concatenate(mixed, axis=1) + b_ref[...]
        o_ref[rows, :] = (u * mixed).astype(o_ref.dtype)


def _gmlp(proj, gmlp_w, gmlp_b, T, ts=512):
    C, G = GMLP_CHUNK, GMLP_GROUPS
    w_cat = gmlp_w.astype(F32).reshape(G // 2, 2, C, C).transpose(0, 2, 1, 3).reshape(G // 2, C, 2 * C)
    b_tab = jnp.repeat(gmlp_b.astype(F32).T, HEAD_DIM, axis=1)
    W = GMLP_WIDTH
    return pl.pallas_call(
        functools.partial(_gmlp_kernel, chunks=ts // C),
        grid=(T // ts,),
        in_specs=[pl.BlockSpec((ts, W), lambda i: (i, COL_GU * LANES // W)),
                  pl.BlockSpec((ts, W), lambda i: (i, COL_GV * LANES // W)),
                  pl.BlockSpec((G // 2, C, 2 * C), lambda i: (0, 0, 0)),
                  pl.BlockSpec((C, W), lambda i: (0, 0))],
        out_specs=pl.BlockSpec((ts, W), lambda i: (i, 0)),
        out_shape=jax.ShapeDtypeStruct((T, W), BF16),
        compiler_params=_params("parallel"),
        name="gmlp",
    )(proj, proj, w_cat, b_tab)


def _outproj_kernel(x_ref, r_ref, m_ref, g_ref, w_ref, o_ref):
    mix = jnp.concatenate([r_ref[...], m_ref[...], g_ref[...]], axis=1)
    o_ref[...] = x_ref[...] + _dot(mix, w_ref[...])


def _outproj(x2, ret, moba, gm, w_out_bf16, tm=512):
    T, D = x2.shape
    row = lambda w: pl.BlockSpec((tm, w), lambda i: (i, 0))
    return pl.pallas_call(
        _outproj_kernel,
        grid=(T // tm,),
        in_specs=[row(D), row(RET_WIDTH), row(MOBA_WIDTH), row(GMLP_WIDTH),
                  pl.BlockSpec(w_out_bf16.shape, lambda i: (0, 0))],
        out_specs=row(D),
        out_shape=jax.ShapeDtypeStruct((T, D), F32),
        compiler_params=_params("parallel"),
        name="outproj",
    )(x2, ret, moba, gm, w_out_bf16)


def _ffn_kernel(x_ref, g_ref, wg_ref, wu_ref, wd_ref, o_ref, h_ref):
    f = pl.program_id(1)

    @pl.when(f == 0)
    def _():
        x = x_ref[...]
        h_ref[...] = _rmsnorm(x, g_ref[...]).astype(BF16)
        o_ref[...] = x

    h = h_ref[...]
    a = _silu(_dot(h, wg_ref[...])) * _dot(h, wu_ref[...])
    o_ref[...] += _dot(a.astype(BF16), wd_ref[...])


def _single(block_shape, index_map):
    return pl.BlockSpec(block_shape, index_map, pipeline_mode=pl.Buffered(1))


def _ffn(x2, g, wg, wu, wd, tm=FFN_TOKENS, tf=FFN_COLS):
    T, D = x2.shape
    F = wg.shape[1]
    return pl.pallas_call(
        _ffn_kernel,
        grid=(T // tm, F // tf),
        in_specs=[_single((tm, D), lambda i, f: (i, 0)),
                  pl.BlockSpec((1, D), lambda i, f: (0, 0)),
                  pl.BlockSpec((D, tf), lambda i, f: (0, f)),
                  pl.BlockSpec((D, tf), lambda i, f: (0, f)),
                  pl.BlockSpec((tf, D), lambda i, f: (f, 0))],
        out_specs=_single((tm, D), lambda i, f: (i, 0)),
        out_shape=jax.ShapeDtypeStruct((T, D), F32),
        scratch_shapes=[pltpu.VMEM((tm, D), BF16)],
        compiler_params=_params("parallel", "arbitrary"),
        name="ffn",
    )(x2, g.reshape(1, D), wg, wu, wd)


def _moe_kernel(x_ref, g_ref, r_ref, wg_ref, wu_ref, wd_ref, fg_ref, o_ref,
                h_ref, gates_ref, gcol_ref):
    e = pl.program_id(1)
    f = pl.program_id(2)
    tm = x_ref.shape[0]
    tf = wg_ref.shape[2]
    lane = lax.broadcasted_iota(jnp.int32, (tm, LANES), 1)

    @pl.when((e == 0) & (f == 0))
    def _():
        x = x_ref[...]
        h = _rmsnorm(x, g_ref[...]).astype(BF16)
        h_ref[...] = h
        o_ref[...] = x
        logits = jnp.where(lane < N_EXPERTS, _dot(h, r_ref[...]), NEG_INF)
        m1 = jnp.max(logits, axis=1, keepdims=True)
        i1 = jnp.min(jnp.where(logits == m1, lane, LANES), axis=1, keepdims=True)
        rest = jnp.where(lane == i1, NEG_INF, logits)
        m2 = jnp.max(rest, axis=1, keepdims=True)
        i2 = jnp.min(jnp.where(rest == m2, lane, LANES), axis=1, keepdims=True)
        e2 = jnp.exp(m2 - m1)
        g1 = 1.0 / (1.0 + e2)
        g2 = e2 / (1.0 + e2)
        gates_ref[...] = jnp.where(lane == i1, g1, 0.0) + jnp.where(lane == i2, g2, 0.0)

    @pl.when(f == 0)
    def _():
        col = jnp.sum(jnp.where(lane == e, gates_ref[...], 0.0), axis=1, keepdims=True)
        gcol_ref[...] = jnp.broadcast_to(col, (tm, LANES))

    h = h_ref[...]
    a = _silu(_dot(h, wg_ref[0])) * _dot(h, wu_ref[0])
    a = a * jnp.concatenate([gcol_ref[...]] * (tf // LANES), axis=1)
    o_ref[...] += _dot(a.astype(BF16), wd_ref[0])

    @pl.when((e == pl.num_programs(1) - 1) & (f == pl.num_programs(2) - 1))
    def _():
        o_ref[...] = _rmsnorm(o_ref[...], fg_ref[...])


def _moe(x2, g, router, wg, wu, wd, final_g, tm=FFN_TOKENS, tf=FFN_COLS):
    T, D = x2.shape
    E, _, F = wg.shape
    r_pad = jnp.zeros((D, LANES), BF16).at[:, :E].set(router.astype(BF16))
    return pl.pallas_call(
        _moe_kernel,
        grid=(T // tm, E, F // tf),
        in_specs=[_single((tm, D), lambda i, e, f: (i, 0)),
                  pl.BlockSpec((1, D), lambda i, e, f: (0, 0)),
                  pl.BlockSpec((D, LANES), lambda i, e, f: (0, 0)),
                  pl.BlockSpec((1, D, tf), lambda i, e, f: (e, 0, f)),
                  pl.BlockSpec((1, D, tf), lambda i, e, f: (e, 0, f)),
                  pl.BlockSpec((1, tf, D), lambda i, e, f: (e, f, 0)),
                  pl.BlockSpec((1, D), lambda i, e, f: (0, 0))],
        out_specs=_single((tm, D), lambda i, e, f: (i, 0)),
        out_shape=jax.ShapeDtypeStruct((T, D), F32),
        scratch_shapes=[pltpu.VMEM((tm, D), BF16),
                        pltpu.VMEM((tm, LANES), F32), pltpu.VMEM((tm, LANES), F32)],
        compiler_params=_params("parallel", "arbitrary", "arbitrary"),
        name="moe",
    )(x2, g.reshape(1, D), r_pad, wg, wu, wd, final_g.reshape(1, D))


def _permute_w_in(w_in):
    n_ret, n_moba = 4 * RET_WIDTH, 3 * MOBA_WIDTH
    gm = w_in[:, n_ret + n_moba:]
    return jnp.concatenate([gm, w_in[:, :n_ret + n_moba]], axis=1).astype(BF16)


def _mixer(x2, B, S, norm_g, w_in, w_out, gmlp_w, gmlp_b):
    T = B * S
    proj = _proj(x2, norm_g, _permute_w_in(w_in))
    ret = _retention(proj, B, S)
    moba = _moba(proj, B, S)
    gm = _gmlp(proj, gmlp_w, gmlp_b, T)
    return _outproj(x2, ret, moba, gm, w_out.astype(BF16))


def kernel(x, l0_mix_norm, l0_w_in, l0_w_out, l0_gmlp_w, l0_gmlp_b, l0_ffn_norm, l0_w_gate, l0_w_up, l0_w_down, l1_mix_norm, l1_w_in, l1_w_out, l1_gmlp_w, l1_gmlp_b, l1_ffn_norm, l1_router, l1_we_gate, l1_we_up, l1_we_down, final_norm):
    B, S, D = x.shape
    x2 = x.reshape(B * S, D)
    x2 = _mixer(x2, B, S, l0_mix_norm, l0_w_in, l0_w_out, l0_gmlp_w, l0_gmlp_b)
    x2 = _ffn(x2, l0_ffn_norm, l0_w_gate.astype(BF16), l0_w_up.astype(BF16), l0_w_down.astype(BF16))
    x2 = _mixer(x2, B, S, l1_mix_norm, l1_w_in, l1_w_out, l1_gmlp_w, l1_gmlp_b)
    x2 = _moe(x2, l1_ffn_norm, l1_router, l1_we_gate.astype(BF16), l1_we_up.astype(BF16),
              l1_we_down.astype(BF16), final_norm)
    return x2.reshape(B, S, D)
```

```python
import functools
import math

import jax
import jax.numpy as jnp
import numpy as np
from jax import lax
from jax.experimental import pallas as pl
from jax.experimental.pallas import tpu as pltpu

F32 = jnp.float32
BF16 = jnp.bfloat16

HEAD_DIM = 64
RET_HEADS = 6
MOBA_HEADS = 6
GMLP_GROUPS = 4
RET_WIDTH = RET_HEADS * HEAD_DIM
MOBA_WIDTH = MOBA_HEADS * HEAD_DIM
GMLP_WIDTH = GMLP_GROUPS * HEAD_DIM
RET_CHUNK = 128
MOBA_BLOCK = 256
MOBA_TOPK = 3
GMLP_CHUNK = 128
N_EXPERTS = 8
NORM_EPS = 1e-6
GN_EPS = 1e-5

LANES = 128
HEAD_PAIRS = RET_HEADS // 2
VMEM_LIMIT = 56 * 1024 * 1024
FFN_TOKENS = 2048
FFN_COLS = 512

COL_GU, COL_GV = 0, 2
COL_RQ, COL_RK, COL_RV, COL_RG = 4, 7, 10, 13
COL_MQ, COL_MK, COL_MV = 16, 19, 22
IN_WIDTH = 25 * LANES

NEG_INF = float("-inf")


def _params(*sem):
    return pltpu.CompilerParams(dimension_semantics=sem, vmem_limit_bytes=VMEM_LIMIT)


def _dot(a, b):
    return jnp.dot(a, b, preferred_element_type=F32)


def _dot_nt(a, b):
    return lax.dot_general(a, b, (((1,), (1,)), ((), ())), preferred_element_type=F32)


def _dot_tn(a, b):
    return lax.dot_general(a, b, (((0,), (0,)), ((), ())), preferred_element_type=F32)


def _rmsnorm(x, g):
    ms = jnp.mean(x * x, axis=-1, keepdims=True)
    return (x * lax.rsqrt(ms + NORM_EPS)) * g


def _silu(x):
    return x / (1.0 + jnp.exp(-x))


def _low_half(shape):
    return lax.broadcasted_iota(jnp.int32, shape, len(shape) - 1) < HEAD_DIM


def _pair_mean(x, low):
    s_lo = jnp.sum(jnp.where(low, x, 0.0), axis=-1, keepdims=True)
    s_hi = jnp.sum(jnp.where(low, 0.0, x), axis=-1, keepdims=True)
    return jnp.where(low, s_lo, s_hi) * (1.0 / HEAD_DIM)


def _proj_kernel(x_ref, g_ref, w_ref, o_ref):
    h = _rmsnorm(x_ref[...], g_ref[...]).astype(BF16)
    o_ref[...] = _dot(h, w_ref[...]).astype(o_ref.dtype)


def _proj(x2, g, w_in_bf16, tm=512):
    T, D = x2.shape
    N = w_in_bf16.shape[1]
    return pl.pallas_call(
        _proj_kernel,
        grid=(T // tm,),
        in_specs=[pl.BlockSpec((tm, D), lambda i: (i, 0)),
                  pl.BlockSpec((1, D), lambda i: (0, 0)),
                  pl.BlockSpec((D, N), lambda i: (0, 0))],
        out_specs=pl.BlockSpec((tm, N), lambda i: (i, 0)),
        out_shape=jax.ShapeDtypeStruct((T, N), BF16),
        compiler_params=_params("parallel"),
        name="proj",
    )(x2, g.reshape(1, D), w_in_bf16)


def _retention_tables():
    C, H, D = RET_CHUNK, RET_HEADS, HEAD_DIM
    log_g = np.log(1.0 - 2.0 ** (-5.0 - np.arange(H, dtype=np.float64)))
    pos = np.arange(C, dtype=np.float64)
    diff = pos[:, None] - pos[None, :]
    decay = np.where(diff >= 0, np.exp(log_g[:, None, None] * np.maximum(diff, 0.0)), 0.0)
    decay = decay * D ** -0.5
    zeta = np.exp(log_g[:, None] * (C - 1 - pos)) * D ** -0.5
    xi = np.exp(log_g[:, None] * (pos + 1))
    cdec = np.exp(log_g * C)

    def lanes(t):
        return np.repeat(t.reshape(HEAD_PAIRS, 2, C).transpose(0, 2, 1), D, axis=2)

    cd = np.repeat(cdec.reshape(HEAD_PAIRS, 1, 2), D, axis=2)
    f = lambda a: jnp.asarray(a, F32)
    return f(decay.reshape(HEAD_PAIRS, 2, C, C)), f(lanes(zeta)), f(lanes(xi)), f(cd)


def _retention_kernel(q_ref, k_ref, v_ref, g_ref, dec_ref, zeta_ref, xi_ref, cd_ref, o_ref,
                      state_ref, *, chunks):
    C = RET_CHUNK

    @pl.when(pl.program_id(2) == 0)
    def _():
        state_ref[...] = jnp.zeros_like(state_ref)

    low = _low_half((C, LANES))
    r = lax.broadcasted_iota(jnp.int32, (LANES, LANES), 0) < HEAD_DIM
    c = lax.broadcasted_iota(jnp.int32, (LANES, LANES), 1) < HEAD_DIM
    same_head = r == c
    zeta = zeta_ref[0]
    xi = xi_ref[0]
    cd = cd_ref[0]
    zero = jnp.zeros((), BF16)

    for ci in range(chunks):
        rows = pl.ds(ci * C, C)
        q = q_ref[rows, :]
        k = k_ref[rows, :]
        v = v_ref[rows, :]
        state = state_ref[...]
        p0 = (_dot_nt(jnp.where(low, q, zero), k) * dec_ref[0, 0]).astype(BF16)
        p1 = (_dot_nt(jnp.where(low, zero, q), k) * dec_ref[0, 1]).astype(BF16)
        intra = _dot(p0, jnp.where(low, v, zero)) + _dot(p1, jnp.where(low, zero, v))
        cross = _dot((q.astype(F32) * xi).astype(BF16), state.astype(BF16))
        o = intra + cross
        kz = (k.astype(F32) * zeta).astype(BF16)
        kv = _dot_tn(kz, v)
        state_ref[...] = state * cd + jnp.where(same_head, kv, 0.0)
        mu = _pair_mean(o, low)
        d = o - mu
        var = _pair_mean(d * d, low)
        on = d * lax.rsqrt(var + GN_EPS)
        o_ref[rows, :] = (on * _silu(g_ref[rows, :].astype(F32))).astype(o_ref.dtype)


def _retention(proj, B, S, ts=512):
    T = B * S
    nst = S // ts
    dec, zeta, xi, cd = _retention_tables()
    C = RET_CHUNK

    def col(c0):
        return pl.BlockSpec((ts, LANES), lambda b, p, s: (b * nst + s, c0 + p))

    return pl.pallas_call(
        functools.partial(_retention_kernel, chunks=ts // C),
        grid=(B, HEAD_PAIRS, nst),
        in_specs=[col(COL_RQ), col(COL_RK), col(COL_RV), col(COL_RG),
                  pl.BlockSpec((1, 2, C, C), lambda b, p, s: (p, 0, 0, 0)),
                  pl.BlockSpec((1, C, LANES), lambda b, p, s: (p, 0, 0)),
                  pl.BlockSpec((1, C, LANES), lambda b, p, s: (p, 0, 0)),
                  pl.BlockSpec((1, 1, LANES), lambda b, p, s: (p, 0, 0))],
        out_specs=pl.BlockSpec((ts, LANES), lambda b, p, s: (b * nst + s, p)),
        out_shape=jax.ShapeDtypeStruct((T, RET_WIDTH), BF16),
        scratch_shapes=[pltpu.VMEM((LANES, LANES), F32)],
        compiler_params=_params("parallel", "parallel", "arbitrary"),
        name="retention",
    )(proj, proj, proj, proj, dec, zeta, xi, cd)


AUG_SLOPE_ROWS = 6
AUG_BIAS_ROW0 = 16
MASK_BIAS = -1e30


def _moba_kernel(slope_ref, q_ref, k_ref, v_ref, o_ref,
                 ka_ref, vt_ref, kmean_ref, qa_ref, m_ref, l_ref, acc_ref, *, nblocks):
    BS = MOBA_BLOCK
    p = pl.program_id(1)
    qi = pl.program_id(2)

    @pl.when(qi == 0)
    def _():
        lane = lax.broadcasted_iota(jnp.int32, (BS, LANES), 1)
        low = lane < HEAD_DIM
        offs = lax.broadcasted_iota(jnp.int32, (BS, LANES), 0).astype(F32)
        kmean_ref[...] = jnp.zeros_like(kmean_ref)
        for jb in range(nblocks):
            kb = k_ref[jb * BS:(jb + 1) * BS, :]
            for hh in range(2):
                a = lane - HEAD_DIM if hh == 0 else lane
                aug = jnp.where(a < 3, offs,
                                jnp.where(a < AUG_SLOPE_ROWS, float(jb * BS),
                                          jnp.where(a == AUG_BIAS_ROW0 + jb, 1.0, 0.0))).astype(BF16)
                ka_ref[hh, jb] = jnp.where(low, kb, aug) if hh == 0 else jnp.where(low, aug, kb)
            kmean_ref[jb:jb + 1, :] = jnp.mean(kb.astype(F32), axis=0, keepdims=True)
            vt_ref[jb] = v_ref[jb * BS:(jb + 1) * BS, :].astype(F32).T.astype(BF16)

    q_t = (q_ref[...].astype(F32) * HEAD_DIM ** -0.5).T
    kmean = kmean_ref[...].astype(BF16)
    r16 = lax.broadcasted_iota(jnp.int32, (AUG_BIAS_ROW0, BS), 0)
    key_off = lax.broadcasted_iota(jnp.int32, (BS, BS), 0)
    qry_off = lax.broadcasted_iota(jnp.int32, (BS, BS), 1)
    causal = key_off <= qry_off
    zeros_h = jnp.zeros((HEAD_DIM, BS), F32)
    pad = jnp.zeros((HEAD_DIM - 2 * AUG_BIAS_ROW0, BS), F32)

    piece = r16 % 3
    qhs, srows = [], []
    for hh in range(2):
        qhs.append(q_t[hh * HEAD_DIM:(hh + 1) * HEAD_DIM, :])
        base = 3 * (2 * p + hh)
        s1, s2, s3 = slope_ref[base], slope_ref[base + 1], slope_ref[base + 2]
        srows.append(jnp.where(r16 < AUG_SLOPE_ROWS,
                               jnp.where(piece == 0, s1, jnp.where(piece == 1, s2, s3)), 0.0))

    def with_aug(hh, bias_rows):
        aug = jnp.concatenate([srows[hh], bias_rows, pad], axis=0)
        return jnp.concatenate([qhs[hh], aug] if hh == 0 else [aug, qhs[hh]], axis=0).astype(BF16)

    qm_both = jnp.concatenate([jnp.concatenate([qhs[0], zeros_h], axis=0),
                               jnp.concatenate([zeros_h, qhs[1]], axis=0)], axis=1).astype(BF16)
    bs_both = _dot(kmean, qm_both)[0:AUG_BIAS_ROW0, :]
    own_bias = jnp.where(r16 == qi, 0.0, MASK_BIAS)
    s_own = [jnp.where(causal, _dot(ka_ref[hh, qi], with_aug(hh, own_bias)), NEG_INF)
             for hh in range(2)]

    for hh in range(2):
        sc = jnp.where(r16 < qi, bs_both[:, hh * BS:(hh + 1) * BS], NEG_INF)
        bias = jnp.full((AUG_BIAS_ROW0, BS), MASK_BIAS, F32)
        for _ in range(MOBA_TOPK):
            mx = jnp.max(sc, axis=0, keepdims=True)
            idx = jnp.min(jnp.where(sc == mx, r16, AUG_BIAS_ROW0), axis=0, keepdims=True)
            pick = (r16 == idx) & (mx > NEG_INF)
            bias = jnp.where(pick, 0.0, bias)
            sc = jnp.where(pick, NEG_INF, sc)
        qa_ref[hh] = with_aug(hh, bias)

    n_pairs = (qi + 1) // 2

    def pair_scores(jj):
        j0 = 2 * jj
        out = []
        for hh in range(2):
            ka = jnp.concatenate([ka_ref[hh, j0], ka_ref[hh, j0 + 1]], axis=0)
            out.append(_dot(ka, qa_ref[hh]))
        return tuple(out)

    first = pair_scores(0)

    for hh in range(2):
        s = s_own[hh]
        m = jnp.max(s, axis=0, keepdims=True)
        e = jnp.exp(s - m)
        m_ref[hh] = m
        l_ref[hh] = jnp.sum(e, axis=0, keepdims=True)
        acc_ref[hh] = _dot(vt_ref[qi, hh * HEAD_DIM:(hh + 1) * HEAD_DIM, :], e.astype(BF16))

    def body(jj, ss):
        nxt = pair_scores(jnp.minimum(jj + 1, n_pairs - 1))
        j0 = 2 * jj
        for hh in range(2):
            s = ss[hh]
            m_old = m_ref[hh]
            m_new = jnp.maximum(m_old, jnp.max(s, axis=0, keepdims=True))
            alpha = jnp.exp(m_old - m_new)
            e = jnp.exp(s - m_new)
            m_ref[hh] = m_new
            l_ref[hh] = l_ref[hh] * alpha + jnp.sum(e, axis=0, keepdims=True)
            rows = slice(hh * HEAD_DIM, (hh + 1) * HEAD_DIM)
            vt = jnp.concatenate([vt_ref[j0, rows, :], vt_ref[j0 + 1, rows, :]], axis=1)
            acc_ref[hh] = acc_ref[hh] * alpha + _dot(vt, e.astype(BF16))
        return nxt

    lax.fori_loop(0, n_pairs, body, first)
    out_t = jnp.concatenate([acc_ref[0] / l_ref[0], acc_ref[1] / l_ref[1]], axis=0)
    o_ref[...] = out_t.T.astype(o_ref.dtype)


def _moba(proj, B, S):
    T = B * S
    BS = MOBA_BLOCK
    nb = S // BS
    slopes = jnp.asarray(2.0 ** (-8.0 * np.arange(1, MOBA_HEADS + 1) / MOBA_HEADS), F32)
    s1 = slopes.astype(BF16).astype(F32)
    s2 = (slopes - s1).astype(BF16).astype(F32)
    s3 = (slopes - s1 - s2).astype(BF16).astype(F32)
    slopes = jnp.stack([s1, s2, s3], axis=1).reshape(-1)
    return pl.pallas_call(
        functools.partial(_moba_kernel, nblocks=nb),
        grid=(B, HEAD_PAIRS, nb),
        in_specs=[pl.BlockSpec(memory_space=pltpu.SMEM),
                  pl.BlockSpec((BS, LANES), lambda b, p, i: (b * nb + i, COL_MQ + p)),
                  pl.BlockSpec((S, LANES), lambda b, p, i: (b, COL_MK + p)),
                  pl.BlockSpec((S, LANES), lambda b, p, i: (b, COL_MV + p))],
        out_specs=pl.BlockSpec((BS, LANES), lambda b, p, i: (b * nb + i, p)),
        out_shape=jax.ShapeDtypeStruct((T, MOBA_WIDTH), BF16),
        scratch_shapes=[pltpu.VMEM((2, nb, BS, LANES), BF16),
                        pltpu.VMEM((nb, LANES, BS), BF16),
                        pltpu.VMEM((LANES, LANES), F32),
                        pltpu.VMEM((2, LANES, BS), BF16),
                        pltpu.VMEM((2, 1, BS), F32),
                        pltpu.VMEM((2, 1, BS), F32),
                        pltpu.VMEM((2, HEAD_DIM, BS), F32)],
        compiler_params=_params("parallel", "parallel", "arbitrary"),
        name="moba",
    )(slopes, proj, proj, proj)


def _gmlp_kernel(u_ref, v_ref, w_ref, b_ref, o_ref, *, chunks):
    C = GMLP_CHUNK
    low = _low_half((C, LANES))
    r = lax.broadcasted_iota(jnp.int32, (C, 2 * C), 0)
    c = lax.broadcasted_iota(jnp.int32, (C, 2 * C), 1)
    tril = jnp.where(c >= C, c - C, c) <= r
    ws = [jnp.where(tril, w_ref[pr], 0.0).astype(BF16) for pr in range(2)]
    for ci in range(chunks):
        rows = pl.ds(ci * C, C)
        u = jax.nn.gelu(u_ref[rows, :].astype(F32))
        vf = jax.nn.gelu(v_ref[rows, :].astype(F32))
        mu = jnp.mean(vf, axis=-1, keepdims=True)
        d = vf - mu
        var = jnp.mean(d * d, axis=-1, keepdims=True)
        vn = d * lax.rsqrt(var + GN_EPS)
        mixed = []
        for pr in range(2):
            vp = vn[:, pr * LANES:(pr + 1) * LANES]
            stacked = jnp.concatenate([jnp.where(low, vp, 0.0), jnp.where(low, 0.0, vp)], axis=0)
            mixed.append(_dot(ws[pr], stacked.astype(BF16)))
        mixed = jnp.concatenate(mixed, axis=1) + b_ref[...]
        o_ref[rows, :] = (u * mixed).astype(o_ref.dtype)


def _gmlp(proj, gmlp_w, gmlp_b, T, ts=512):
    C, G = GMLP_CHUNK, GMLP_GROUPS
    w_cat = gmlp_w.astype(F32).reshape(G // 2, 2, C, C).transpose(0, 2, 1, 3).reshape(G // 2, C, 2 * C)
    b_tab = jnp.repeat(gmlp_b.astype(F32).T, HEAD_DIM, axis=1)
    W = GMLP_WIDTH
    return pl.pallas_call(
        functools.partial(_gmlp_kernel, chunks=ts // C),
        grid=(T // ts,),
        in_specs=[pl.BlockSpec((ts, W), lambda i: (i, COL_GU * LANES // W)),
                  pl.BlockSpec((ts, W), lambda i: (i, COL_GV * LANES // W)),
                  pl.BlockSpec((G // 2, C, 2 * C), lambda i: (0, 0, 0)),
                  pl.BlockSpec((C, W), lambda i: (0, 0))],
        out_specs=pl.BlockSpec((ts, W), lambda i: (i, 0)),
        out_shape=jax.ShapeDtypeStruct((T, W), BF16),
        compiler_params=_params("parallel"),
        name="gmlp",
    )(proj, proj, w_cat, b_tab)


def _outproj_kernel(x_ref, r_ref, m_ref, g_ref, w_ref, o_ref):
    mix = jnp.concatenate([r_ref[...], m_ref[...], g_ref[...]], axis=1)
    o_ref[...] = x_ref[...] + _dot(mix, w_ref[...])


def _outproj(x2, ret, moba, gm, w_out_bf16, tm=512):
    T, D = x2.shape
    row = lambda w: pl.BlockSpec((tm, w), lambda i: (i, 0))
    return pl.pallas_call(
        _outproj_kernel,
        grid=(T // tm,),
        in_specs=[row(D), row(RET_WIDTH), row(MOBA_WIDTH), row(GMLP_WIDTH),
                  pl.BlockSpec(w_out_bf16.shape, lambda i: (0, 0))],
        out_specs=row(D),
        out_shape=jax.ShapeDtypeStruct((T, D), F32),
        compiler_params=_params("parallel"),
        name="outproj",
    )(x2, ret, moba, gm, w_out_bf16)


def _ffn_kernel(x_ref, g_ref, wg_ref, wu_ref, wd_ref, o_ref, h_ref):
    f = pl.program_id(1)

    @pl.when(f == 0)
    def _():
        x = x_ref[...]
        h_ref[...] = _rmsnorm(x, g_ref[...]).astype(BF16)
        o_ref[...] = x

    h = h_ref[...]
    a = _silu(_dot(h, wg_ref[...])) * _dot(h, wu_ref[...])
    o_ref[...] += _dot(a.astype(BF16), wd_ref[...])


def _single(block_shape, index_map):
    return pl.BlockSpec(block_shape, index_map, pipeline_mode=pl.Buffered(1))


def _ffn(x2, g, wg, wu, wd, tm=FFN_TOKENS // 2, tf=FFN_COLS):
    T, D = x2.shape
    F = wg.shape[1]
    return pl.pallas_call(
        _ffn_kernel,
        grid=(T // tm, F // tf),
        in_specs=[pl.BlockSpec((tm, D), lambda i, f: (i, 0)),
                  pl.BlockSpec((1, D), lambda i, f: (0, 0)),
                  pl.BlockSpec((D, tf), lambda i, f: (0, f)),
                  pl.BlockSpec((D, tf), lambda i, f: (0, f)),
                  pl.BlockSpec((tf, D), lambda i, f: (f, 0))],
        out_specs=pl.BlockSpec((tm, D), lambda i, f: (i, 0)),
        out_shape=jax.ShapeDtypeStruct((T, D), F32),
        scratch_shapes=[pltpu.VMEM((tm, D), BF16)],
        compiler_params=_params("parallel", "arbitrary"),
        name="ffn",
    )(x2, g.reshape(1, D), wg, wu, wd)


def _moe_kernel(x_ref, g_ref, r_ref, wg_ref, wu_ref, wd_ref, fg_ref, o_ref,
                h_ref, gates_ref, gcol_ref):
    e = pl.program_id(1)
    f = pl.program_id(2)
    tm = x_ref.shape[0]
    tf = wg_ref.shape[2]
    lane = lax.broadcasted_iota(jnp.int32, (tm, LANES), 1)

    @pl.when((e == 0) & (f == 0))
    def _():
        x = x_ref[...]
        h = _rmsnorm(x, g_ref[...]).astype(BF16)
        h_ref[...] = h
        o_ref[...] = x
        logits = jnp.where(lane < N_EXPERTS, _dot(h, r_ref[...]), NEG_INF)
        m1 = jnp.max(logits, axis=1, keepdims=True)
        i1 = jnp.min(jnp.where(logits == m1, lane, LANES), axis=1, keepdims=True)
        rest = jnp.where(lane == i1, NEG_INF, logits)
        m2 = jnp.max(rest, axis=1, keepdims=True)
        i2 = jnp.min(jnp.where(rest == m2, lane, LANES), axis=1, keepdims=True)
        e2 = jnp.exp(m2 - m1)
        g1 = 1.0 / (1.0 + e2)
        g2 = e2 / (1.0 + e2)
        gates_ref[...] = jnp.where(lane == i1, g1, 0.0) + jnp.where(lane == i2, g2, 0.0)

    @pl.when(f == 0)
    def _():
        col = jnp.sum(jnp.where(lane == e, gates_ref[...], 0.0), axis=1, keepdims=True)
        gcol_ref[...] = jnp.broadcast_to(col, (tm, LANES))

    h = h_ref[...]
    a = _silu(_dot(h, wg_ref[0])) * _dot(h, wu_ref[0])
    a = a * jnp.concatenate([gcol_ref[...]] * (tf // LANES), axis=1)
    o_ref[...] += _dot(a.astype(BF16), wd_ref[0])

    @pl.when((e == pl.num_programs(1) - 1) & (f == pl.num_programs(2) - 1))
    def _():
        o_ref[...] = _rmsnorm(o_ref[...], fg_ref[...])


def _moe(x2, g, router, wg, wu, wd, final_g, tm=FFN_TOKENS, tf=FFN_COLS):
    T, D = x2.shape
    E, _, F = wg.shape
    r_pad = jnp.zeros((D, LANES), BF16).at[:, :E].set(router.astype(BF16))
    return pl.pallas_call(
        _moe_kernel,
        grid=(T // tm, E, F // tf),
        in_specs=[_single((tm, D), lambda i, e, f: (i, 0)),
                  pl.BlockSpec((1, D), lambda i, e, f: (0, 0)),
                  pl.BlockSpec((D, LANES), lambda i, e, f: (0, 0)),
                  pl.BlockSpec((1, D, tf), lambda i, e, f: (e, 0, f)),
                  pl.BlockSpec((1, D, tf), lambda i, e, f: (e, 0, f)),
                  pl.BlockSpec((1, tf, D), lambda i, e, f: (e, f, 0)),
                  pl.BlockSpec((1, D), lambda i, e, f: (0, 0))],
        out_specs=_single((tm, D), lambda i, e, f: (i, 0)),
        out_shape=jax.ShapeDtypeStruct((T, D), F32),
        scratch_shapes=[pltpu.VMEM((tm, D), BF16),
                        pltpu.VMEM((tm, LANES), F32), pltpu.VMEM((tm, LANES), F32)],
        compiler_params=_params("parallel", "arbitrary", "arbitrary"),
        name="moe",
    )(x2, g.reshape(1, D), r_pad, wg, wu, wd, final_g.reshape(1, D))


def _permute_w_in(w_in):
    n_ret, n_moba = 4 * RET_WIDTH, 3 * MOBA_WIDTH
    gm = w_in[:, n_ret + n_moba:]
    return jnp.concatenate([gm, w_in[:, :n_ret + n_moba]], axis=1).astype(BF16)


def _mixer(x2, B, S, norm_g, w_in, w_out, gmlp_w, gmlp_b):
    T = B * S
    proj = _proj(x2, norm_g, _permute_w_in(w_in))
    ret = _retention(proj, B, S)
    moba = _moba(proj, B, S)
    gm = _gmlp(proj, gmlp_w, gmlp_b, T)
    return _outproj(x2, ret, moba, gm, w_out.astype(BF16))


def kernel(x, l0_mix_norm, l0_w_in, l0_w_out, l0_gmlp_w, l0_gmlp_b, l0_ffn_norm, l0_w_gate, l0_w_up, l0_w_down, l1_mix_norm, l1_w_in, l1_w_out, l1_gmlp_w, l1_gmlp_b, l1_ffn_norm, l1_router, l1_we_gate, l1_we_up, l1_we_down, final_norm):
    B, S, D = x.shape
    x2 = x.reshape(B * S, D)
    x2 = _mixer(x2, B, S, l0_mix_norm, l0_w_in, l0_w_out, l0_gmlp_w, l0_gmlp_b)
    x2 = _ffn(x2, l0_ffn_norm, l0_w_gate.astype(BF16), l0_w_up.astype(BF16), l0_w_down.astype(BF16))
    x2 = _mixer(x2, B, S, l1_mix_norm, l1_w_in, l1_w_out, l1_gmlp_w, l1_gmlp_b)
    x2 = _moe(x2, l1_ffn_norm, l1_router, l1_we_gate.astype(BF16), l1_we_up.astype(BF16),
              l1_we_down.astype(BF16), final_norm)
    return x2.reshape(B, S, D)
```

```python
import functools
import math

import jax
import jax.numpy as jnp
import numpy as np
from jax import lax
from jax.experimental import pallas as pl
from jax.experimental.pallas import tpu as pltpu

F32 = jnp.float32
BF16 = jnp.bfloat16

HEAD_DIM = 64
RET_HEADS = 6
MOBA_HEADS = 6
GMLP_GROUPS = 4
RET_WIDTH = RET_HEADS * HEAD_DIM
MOBA_WIDTH = MOBA_HEADS * HEAD_DIM
GMLP_WIDTH = GMLP_GROUPS * HEAD_DIM
RET_CHUNK = 128
MOBA_BLOCK = 256
MOBA_TOPK = 3
GMLP_CHUNK = 128
N_EXPERTS = 8
NORM_EPS = 1e-6
GN_EPS = 1e-5

LANES = 128
HEAD_PAIRS = RET_HEADS // 2
VMEM_LIMIT = 56 * 1024 * 1024
FFN_TOKENS = 2048
FFN_COLS = 512
FFN_ROW_CHUNKS = 4

COL_GU, COL_GV = 0, 2
COL_RQ, COL_RK, COL_RV, COL_RG = 4, 7, 10, 13
COL_MQ, COL_MK, COL_MV = 16, 19, 22
IN_WIDTH = 25 * LANES

NEG_INF = float("-inf")


def _params(*sem):
    return pltpu.CompilerParams(dimension_semantics=sem, vmem_limit_bytes=VMEM_LIMIT)


def _dot(a, b):
    return jnp.dot(a, b, preferred_element_type=F32)


def _dot_nt(a, b):
    return lax.dot_general(a, b, (((1,), (1,)), ((), ())), preferred_element_type=F32)


def _dot_tn(a, b):
    return lax.dot_general(a, b, (((0,), (0,)), ((), ())), preferred_element_type=F32)


def _rmsnorm(x, g):
    ms = jnp.mean(x * x, axis=-1, keepdims=True)
    return (x * lax.rsqrt(ms + NORM_EPS)) * g


def _silu(x):
    return x / (1.0 + jnp.exp(-x))


def _low_half(shape):
    return lax.broadcasted_iota(jnp.int32, shape, len(shape) - 1) < HEAD_DIM


def _pair_mean(x, low):
    s_lo = jnp.sum(jnp.where(low, x, 0.0), axis=-1, keepdims=True)
    s_hi = jnp.sum(jnp.where(low, 0.0, x), axis=-1, keepdims=True)
    return jnp.where(low, s_lo, s_hi) * (1.0 / HEAD_DIM)


def _proj_kernel(x_ref, g_ref, w_ref, o_ref):
    h = _rmsnorm(x_ref[...], g_ref[...]).astype(BF16)
    o_ref[...] = _dot(h, w_ref[...]).astype(o_ref.dtype)


def _proj(x2, g, w_in_bf16, tm=512):
    T, D = x2.shape
    N = w_in_bf16.shape[1]
    return pl.pallas_call(
        _proj_kernel,
        grid=(T // tm,),
        in_specs=[pl.BlockSpec((tm, D), lambda i: (i, 0)),
                  pl.BlockSpec((1, D), lambda i: (0, 0)),
                  pl.BlockSpec((D, N), lambda i: (0, 0))],
        out_specs=pl.BlockSpec((tm, N), lambda i: (i, 0)),
        out_shape=jax.ShapeDtypeStruct((T, N), BF16),
        compiler_params=_params("parallel"),
        name="proj",
    )(x2, g.reshape(1, D), w_in_bf16)


def _retention_tables():
    C, H, D = RET_CHUNK, RET_HEADS, HEAD_DIM
    log_g = np.log(1.0 - 2.0 ** (-5.0 - np.arange(H, dtype=np.float64)))
    pos = np.arange(C, dtype=np.float64)
    diff = pos[:, None] - pos[None, :]
    decay = np.where(diff >= 0, np.exp(log_g[:, None, None] * np.maximum(diff, 0.0)), 0.0)
    decay = decay * D ** -0.5
    zeta = np.exp(log_g[:, None] * (C - 1 - pos)) * D ** -0.5
    xi = np.exp(log_g[:, None] * (pos + 1))
    cdec = np.exp(log_g * C)

    def lanes(t):
        return np.repeat(t.reshape(HEAD_PAIRS, 2, C).transpose(0, 2, 1), D, axis=2)

    cd = np.repeat(cdec.reshape(HEAD_PAIRS, 1, 2), D, axis=2)
    f = lambda a: jnp.asarray(a, F32)
    return f(decay.reshape(HEAD_PAIRS, 2, C, C)), f(lanes(zeta)), f(lanes(xi)), f(cd)


def _retention_kernel(q_ref, k_ref, v_ref, g_ref, dec_ref, zeta_ref, xi_ref, cd_ref, o_ref,
                      state_ref, *, chunks):
    C = RET_CHUNK

    @pl.when(pl.program_id(2) == 0)
    def _():
        state_ref[...] = jnp.zeros_like(state_ref)

    low = _low_half((C, LANES))
    r = lax.broadcasted_iota(jnp.int32, (LANES, LANES), 0) < HEAD_DIM
    c = lax.broadcasted_iota(jnp.int32, (LANES, LANES), 1) < HEAD_DIM
    same_head = r == c
    zeta = zeta_ref[0]
    xi = xi_ref[0]
    cd = cd_ref[0]
    zero = jnp.zeros((), BF16)

    for ci in range(chunks):
        rows = pl.ds(ci * C, C)
        q = q_ref[rows, :]
        k = k_ref[rows, :]
        v = v_ref[rows, :]
        state = state_ref[...]
        p0 = (_dot_nt(jnp.where(low, q, zero), k) * dec_ref[0, 0]).astype(BF16)
        p1 = (_dot_nt(jnp.where(low, zero, q), k) * dec_ref[0, 1]).astype(BF16)
        intra = _dot(p0, jnp.where(low, v, zero)) + _dot(p1, jnp.where(low, zero, v))
        cross = _dot((q.astype(F32) * xi).astype(BF16), state.astype(BF16))
        o = intra + cross
        kz = (k.astype(F32) * zeta).astype(BF16)
        kv = _dot_tn(kz, v)
        state_ref[...] = state * cd + jnp.where(same_head, kv, 0.0)
        mu = _pair_mean(o, low)
        d = o - mu
        var = _pair_mean(d * d, low)
        on = d * lax.rsqrt(var + GN_EPS)
        o_ref[rows, :] = (on * _silu(g_ref[rows, :].astype(F32))).astype(o_ref.dtype)


def _retention(proj, B, S, ts=512):
    T = B * S
    nst = S // ts
    dec, zeta, xi, cd = _retention_tables()
    C = RET_CHUNK

    def col(c0):
        return pl.BlockSpec((ts, LANES), lambda b, p, s: (b * nst + s, c0 + p))

    return pl.pallas_call(
        functools.partial(_retention_kernel, chunks=ts // C),
        grid=(B, HEAD_PAIRS, nst),
        in_specs=[col(COL_RQ), col(COL_RK), col(COL_RV), col(COL_RG),
                  pl.BlockSpec((1, 2, C, C), lambda b, p, s: (p, 0, 0, 0)),
                  pl.BlockSpec((1, C, LANES), lambda b, p, s: (p, 0, 0)),
                  pl.BlockSpec((1, C, LANES), lambda b, p, s: (p, 0, 0)),
                  pl.BlockSpec((1, 1, LANES), lambda b, p, s: (p, 0, 0))],
        out_specs=pl.BlockSpec((ts, LANES), lambda b, p, s: (b * nst + s, p)),
        out_shape=jax.ShapeDtypeStruct((T, RET_WIDTH), BF16),
        scratch_shapes=[pltpu.VMEM((LANES, LANES), F32)],
        compiler_params=_params("parallel", "parallel", "arbitrary"),
        name="retention",
    )(proj, proj, proj, proj, dec, zeta, xi, cd)


AUG_SLOPE_ROWS = 6
AUG_BIAS_ROW0 = 16
MASK_BIAS = -1e30


def _moba_kernel(slope_ref, q_ref, k_ref, v_ref, o_ref,
                 ka_ref, vt_ref, kmean_ref, qa_ref, m_ref, l_ref, acc_ref, *, nblocks):
    BS = MOBA_BLOCK
    p = pl.program_id(1)
    qi = pl.program_id(2)

    @pl.when(qi == 0)
    def _():
        lane = lax.broadcasted_iota(jnp.int32, (BS, LANES), 1)
        low = lane < HEAD_DIM
        offs = lax.broadcasted_iota(jnp.int32, (BS, LANES), 0).astype(F32)
        kmean_ref[...] = jnp.zeros_like(kmean_ref)
        for jb in range(nblocks):
            kb = k_ref[jb * BS:(jb + 1) * BS, :]
            for hh in range(2):
                a = lane - HEAD_DIM if hh == 0 else lane
                aug = jnp.where(a < 3, offs,
                                jnp.where(a < AUG_SLOPE_ROWS, float(jb * BS),
                                          jnp.where(a == AUG_BIAS_ROW0 + jb, 1.0, 0.0))).astype(BF16)
                ka_ref[hh, jb] = jnp.where(low, kb, aug) if hh == 0 else jnp.where(low, aug, kb)
            kmean_ref[jb:jb + 1, :] = jnp.mean(kb.astype(F32), axis=0, keepdims=True)
            vt_ref[jb] = v_ref[jb * BS:(jb + 1) * BS, :].astype(F32).T.astype(BF16)

    q_t = (q_ref[...].astype(F32) * HEAD_DIM ** -0.5).T
    kmean = kmean_ref[...].astype(BF16)
    r16 = lax.broadcasted_iota(jnp.int32, (AUG_BIAS_ROW0, BS), 0)
    key_off = lax.broadcasted_iota(jnp.int32, (BS, BS), 0)
    qry_off = lax.broadcasted_iota(jnp.int32, (BS, BS), 1)
    causal = key_off <= qry_off
    zeros_h = jnp.zeros((HEAD_DIM, BS), F32)
    pad = jnp.zeros((HEAD_DIM - 2 * AUG_BIAS_ROW0, BS), F32)

    piece = r16 % 3
    qhs, srows = [], []
    for hh in range(2):
        qhs.append(q_t[hh * HEAD_DIM:(hh + 1) * HEAD_DIM, :])
        base = 3 * (2 * p + hh)
        s1, s2, s3 = slope_ref[base], slope_ref[base + 1], slope_ref[base + 2]
        srows.append(jnp.where(r16 < AUG_SLOPE_ROWS,
                               jnp.where(piece == 0, s1, jnp.where(piece == 1, s2, s3)), 0.0))

    def with_aug(hh, bias_rows):
        aug = jnp.concatenate([srows[hh], bias_rows, pad], axis=0)
        return jnp.concatenate([qhs[hh], aug] if hh == 0 else [aug, qhs[hh]], axis=0).astype(BF16)

    qm_both = jnp.concatenate([jnp.concatenate([qhs[0], zeros_h], axis=0),
                               jnp.concatenate([zeros_h, qhs[1]], axis=0)], axis=1).astype(BF16)
    bs_both = _dot(kmean, qm_both)[0:AUG_BIAS_ROW0, :]
    own_bias = jnp.where(r16 == qi, 0.0, MASK_BIAS)
    s_own = [jnp.where(causal, _dot(ka_ref[hh, qi], with_aug(hh, own_bias)), NEG_INF)
             for hh in range(2)]

    for hh in range(2):
        sc = jnp.where(r16 < qi, bs_both[:, hh * BS:(hh + 1) * BS], NEG_INF)
        bias = jnp.full((AUG_BIAS_ROW0, BS), MASK_BIAS, F32)
        for _ in range(MOBA_TOPK):
            mx = jnp.max(sc, axis=0, keepdims=True)
            idx = jnp.min(jnp.where(sc == mx, r16, AUG_BIAS_ROW0), axis=0, keepdims=True)
            pick = (r16 == idx) & (mx > NEG_INF)
            bias = jnp.where(pick, 0.0, bias)
            sc = jnp.where(pick, NEG_INF, sc)
        qa_ref[hh] = with_aug(hh, bias)

    n_pairs = (qi + 1) // 2

    def pair_scores(jj):
        j0 = 2 * jj
        out = []
        for hh in range(2):
            ka = jnp.concatenate([ka_ref[hh, j0], ka_ref[hh, j0 + 1]], axis=0)
            out.append(_dot(ka, qa_ref[hh]))
        return tuple(out)

    first = pair_scores(0)

    for hh in range(2):
        s = s_own[hh]
        m = jnp.max(s, axis=0, keepdims=True)
        e = jnp.exp(s - m)
        m_ref[hh] = m
        l_ref[hh] = jnp.sum(e, axis=0, keepdims=True)
        acc_ref[hh] = _dot(vt_ref[qi, hh * HEAD_DIM:(hh + 1) * HEAD_DIM, :], e.astype(BF16))

    def body(jj, ss):
        nxt = pair_scores(jnp.minimum(jj + 1, n_pairs - 1))
        j0 = 2 * jj
        for hh in range(2):
            s = ss[hh]
            m_old = m_ref[hh]
            m_new = jnp.maximum(m_old, jnp.max(s, axis=0, keepdims=True))
            alpha = jnp.exp(m_old - m_new)
            e = jnp.exp(s - m_new)
            m_ref[hh] = m_new
            l_ref[hh] = l_ref[hh] * alpha + jnp.sum(e, axis=0, keepdims=True)
            rows = slice(hh * HEAD_DIM, (hh + 1) * HEAD_DIM)
            vt = jnp.concatenate([vt_ref[j0, rows, :], vt_ref[j0 + 1, rows, :]], axis=1)
            acc_ref[hh] = acc_ref[hh] * alpha + _dot(vt, e.astype(BF16))
        return nxt

    lax.fori_loop(0, n_pairs, body, first)
    out_t = jnp.concatenate([acc_ref[0] / l_ref[0], acc_ref[1] / l_ref[1]], axis=0)
    o_ref[...] = out_t.T.astype(o_ref.dtype)


def _moba(proj, B, S):
    T = B * S
    BS = MOBA_BLOCK
    nb = S // BS
    slopes = jnp.asarray(2.0 ** (-8.0 * np.arange(1, MOBA_HEADS + 1) / MOBA_HEADS), F32)
    s1 = slopes.astype(BF16).astype(F32)
    s2 = (slopes - s1).astype(BF16).astype(F32)
    s3 = (slopes - s1 - s2).astype(BF16).astype(F32)
    slopes = jnp.stack([s1, s2, s3], axis=1).reshape(-1)
    return pl.pallas_call(
        functools.partial(_moba_kernel, nblocks=nb),
        grid=(B, HEAD_PAIRS, nb),
        in_specs=[pl.BlockSpec(memory_space=pltpu.SMEM),
                  pl.BlockSpec((BS, LANES), lambda b, p, i: (b * nb + i, COL_MQ + p)),
                  pl.BlockSpec((S, LANES), lambda b, p, i: (b, COL_MK + p)),
                  pl.BlockSpec((S, LANES), lambda b, p, i: (b, COL_MV + p))],
        out_specs=pl.BlockSpec((BS, LANES), lambda b, p, i: (b * nb + i, p)),
        out_shape=jax.ShapeDtypeStruct((T, MOBA_WIDTH), BF16),
        scratch_shapes=[pltpu.VMEM((2, nb, BS, LANES), BF16),
                        pltpu.VMEM((nb, LANES, BS), BF16),
                        pltpu.VMEM((LANES, LANES), F32),
                        pltpu.VMEM((2, LANES, BS), BF16),
                        pltpu.VMEM((2, 1, BS), F32),
                        pltpu.VMEM((2, 1, BS), F32),
                        pltpu.VMEM((2, HEAD_DIM, BS), F32)],
        compiler_params=_params("parallel", "parallel", "arbitrary"),
        name="moba",
    )(slopes, proj, proj, proj)


def _gmlp_kernel(u_ref, v_ref, w_ref, b_ref, o_ref, *, chunks):
    C = GMLP_CHUNK
    low = _low_half((C, LANES))
    r = lax.broadcasted_iota(jnp.int32, (C, 2 * C), 0)
    c = lax.broadcasted_iota(jnp.int32, (C, 2 * C), 1)
    tril = jnp.where(c >= C, c - C, c) <= r
    ws = [jnp.where(tril, w_ref[pr], 0.0).astype(BF16) for pr in range(2)]
    for ci in range(chunks):
        rows = pl.ds(ci * C, C)
        u = jax.nn.gelu(u_ref[rows, :].astype(F32))
        vf = jax.nn.gelu(v_ref[rows, :].astype(F32))
        mu = jnp.mean(vf, axis=-1, keepdims=True)
        d = vf - mu
        var = jnp.mean(d * d, axis=-1, keepdims=True)
        vn = d * lax.rsqrt(var + GN_EPS)
        mixed = []
        for pr in range(2):
            vp = vn[:, pr * LANES:(pr + 1) * LANES]
            stacked = jnp.concatenate([jnp.where(low, vp, 0.0), jnp.where(low, 0.0, vp)], axis=0)
            mixed.append(_dot(ws[pr], stacked.astype(BF16)))
        mixed = jnp.concatenate(mixed, axis=1) + b_ref[...]
        o_ref[rows, :] = (u * mixed).astype(o_ref.dtype)


def _gmlp(proj, gmlp_w, gmlp_b, T, ts=512):
    C, G = GMLP_CHUNK, GMLP_GROUPS
    w_cat = gmlp_w.astype(F32).reshape(G // 2, 2, C, C).transpose(0, 2, 1, 3).reshape(G // 2, C, 2 * C)
    b_tab = jnp.repeat(gmlp_b.astype(F32).T, HEAD_DIM, axis=1)
    W = GMLP_WIDTH
    return pl.pallas_call(
        functools.partial(_gmlp_kernel, chunks=ts // C),
        grid=(T // ts,),
        in_specs=[pl.BlockSpec((ts, W), lambda i: (i, COL_GU * LANES // W)),
                  pl.BlockSpec((ts, W), lambda i: (i, COL_GV * LANES // W)),
                  pl.BlockSpec((G // 2, C, 2 * C), lambda i: (0, 0, 0)),
                  pl.BlockSpec((C, W), lambda i: (0, 0))],
        out_specs=pl.BlockSpec((ts, W), lambda i: (i, 0)),
        out_shape=jax.ShapeDtypeStruct((T, W), BF16),
        compiler_params=_params("parallel"),
        name="gmlp",
    )(proj, proj, w_cat, b_tab)


def _outproj_kernel(x_ref, r_ref, m_ref, g_ref, w_ref, o_ref):
    mix = jnp.concatenate([r_ref[...], m_ref[...], g_ref[...]], axis=1)
    o_ref[...] = x_ref[...] + _dot(mix, w_ref[...])


def _outproj(x2, ret, moba, gm, w_out_bf16, tm=512):
    T, D = x2.shape
    row = lambda w: pl.BlockSpec((tm, w), lambda i: (i, 0))
    return pl.pallas_call(
        _outproj_kernel,
        grid=(T // tm,),
        in_specs=[row(D), row(RET_WIDTH), row(MOBA_WIDTH), row(GMLP_WIDTH),
                  pl.BlockSpec(w_out_bf16.shape, lambda i: (0, 0))],
        out_specs=row(D),
        out_shape=jax.ShapeDtypeStruct((T, D), F32),
        compiler_params=_params("parallel"),
        name="outproj",
    )(x2, ret, moba, gm, w_out_bf16)


def _swiglu_accumulate(h_ref, wg, wu, wd, o_ref, scale_ref):
    tm = h_ref.shape[0]
    tf = wg.shape[1]
    rc = tm // FFN_ROW_CHUNKS
    wg, wu, wd = wg.astype(BF16), wu.astype(BF16), wd.astype(BF16)

    def gate_up(c):
        h = h_ref[c * rc:(c + 1) * rc, :]
        return _dot(h, wg), _dot(h, wu)

    cur = gate_up(0)
    for c in range(FFN_ROW_CHUNKS):
        nxt = gate_up(c + 1) if c + 1 < FFN_ROW_CHUNKS else None
        rows = slice(c * rc, (c + 1) * rc)
        a = _silu(cur[0]) * cur[1]
        if scale_ref is not None:
            a = a * jnp.concatenate([scale_ref[rows, :]] * (tf // LANES), axis=1)
        o_ref[rows, :] += _dot(a.astype(BF16), wd)
        cur = nxt


def _ffn_kernel(x_ref, g_ref, wg_ref, wu_ref, wd_ref, o_ref, h_ref):
    f = pl.program_id(1)

    @pl.when(f == 0)
    def _():
        x = x_ref[...]
        h_ref[...] = _rmsnorm(x, g_ref[...]).astype(BF16)
        o_ref[...] = x

    _swiglu_accumulate(h_ref, wg_ref[...], wu_ref[...], wd_ref[...], o_ref, None)


def _single(block_shape, index_map):
    return pl.BlockSpec(block_shape, index_map, pipeline_mode=pl.Buffered(1))


def _ffn(x2, g, wg, wu, wd, tm=FFN_TOKENS // 2, tf=FFN_COLS):
    T, D = x2.shape
    F = wg.shape[1]
    return pl.pallas_call(
        _ffn_kernel,
        grid=(T // tm, F // tf),
        in_specs=[pl.BlockSpec((tm, D), lambda i, f: (i, 0)),
                  pl.BlockSpec((1, D), lambda i, f: (0, 0)),
                  pl.BlockSpec((D, tf), lambda i, f: (0, f)),
                  pl.BlockSpec((D, tf), lambda i, f: (0, f)),
                  pl.BlockSpec((tf, D), lambda i, f: (f, 0))],
        out_specs=pl.BlockSpec((tm, D), lambda i, f: (i, 0)),
        out_shape=jax.ShapeDtypeStruct((T, D), F32),
        scratch_shapes=[pltpu.VMEM((tm, D), BF16)],
        compiler_params=_params("parallel", "arbitrary"),
        name="ffn",
    )(x2, g.reshape(1, D), wg, wu, wd)


def _moe_kernel(x_ref, g_ref, r_ref, wg_ref, wu_ref, wd_ref, fg_ref, o_ref,
                h_ref, gates_ref, gcol_ref):
    e = pl.program_id(1)
    f = pl.program_id(2)
    tm = x_ref.shape[0]
    tf = wg_ref.shape[2]
    lane = lax.broadcasted_iota(jnp.int32, (tm, LANES), 1)

    @pl.when((e == 0) & (f == 0))
    def _():
        x = x_ref[...]
        h = _rmsnorm(x, g_ref[...]).astype(BF16)
        h_ref[...] = h
        o_ref[...] = x
        logits = jnp.where(lane < N_EXPERTS, _dot(h, r_ref[...]), NEG_INF)
        m1 = jnp.max(logits, axis=1, keepdims=True)
        i1 = jnp.min(jnp.where(logits == m1, lane, LANES), axis=1, keepdims=True)
        rest = jnp.where(lane == i1, NEG_INF, logits)
        m2 = jnp.max(rest, axis=1, keepdims=True)
        i2 = jnp.min(jnp.where(rest == m2, lane, LANES), axis=1, keepdims=True)
        e2 = jnp.exp(m2 - m1)
        g1 = 1.0 / (1.0 + e2)
        g2 = e2 / (1.0 + e2)
        gates_ref[...] = jnp.where(lane == i1, g1, 0.0) + jnp.where(lane == i2, g2, 0.0)

    @pl.when(f == 0)
    def _():
        col = jnp.sum(jnp.where(lane == e, gates_ref[...], 0.0), axis=1, keepdims=True)
        gcol_ref[...] = jnp.broadcast_to(col, (tm, LANES))

    _swiglu_accumulate(h_ref, wg_ref[0], wu_ref[0], wd_ref[0], o_ref, gcol_ref)

    @pl.when((e == pl.num_programs(1) - 1) & (f == pl.num_programs(2) - 1))
    def _():
        o_ref[...] = _rmsnorm(o_ref[...], fg_ref[...])


def _moe(x2, g, router, wg, wu, wd, final_g, tm=FFN_TOKENS, tf=FFN_COLS):
    T, D = x2.shape
    E, _, F = wg.shape
    r_pad = jnp.zeros((D, LANES), BF16).at[:, :E].set(router.astype(BF16))
    return pl.pallas_call(
        _moe_kernel,
        grid=(T // tm, E, F // tf),
        in_specs=[_single((tm, D), lambda i, e, f: (i, 0)),
                  pl.BlockSpec((1, D), lambda i, e, f: (0, 0)),
                  pl.BlockSpec((D, LANES), lambda i, e, f: (0, 0)),
                  pl.BlockSpec((1, D, tf), lambda i, e, f: (e, 0, f)),
                  pl.BlockSpec((1, D, tf), lambda i, e, f: (e, 0, f)),
                  pl.BlockSpec((1, tf, D), lambda i, e, f: (e, f, 0)),
                  pl.BlockSpec((1, D), lambda i, e, f: (0, 0))],
        out_specs=_single((tm, D), lambda i, e, f: (i, 0)),
        out_shape=jax.ShapeDtypeStruct((T, D), F32),
        scratch_shapes=[pltpu.VMEM((tm, D), BF16),
                        pltpu.VMEM((tm, LANES), F32), pltpu.VMEM((tm, LANES), F32)],
        compiler_params=_params("parallel", "arbitrary", "arbitrary"),
        name="moe",
    )(x2, g.reshape(1, D), r_pad, wg, wu, wd, final_g.reshape(1, D))


def _permute_w_in(w_in):
    n_ret, n_moba = 4 * RET_WIDTH, 3 * MOBA_WIDTH
    gm = w_in[:, n_ret + n_moba:]
    return jnp.concatenate([gm, w_in[:, :n_ret + n_moba]], axis=1).astype(BF16)


def _mixer(x2, B, S, norm_g, w_in, w_out, gmlp_w, gmlp_b):
    T = B * S
    proj = _proj(x2, norm_g, _permute_w_in(w_in))
    ret = _retention(proj, B, S)
    moba = _moba(proj, B, S)
    gm = _gmlp(proj, gmlp_w, gmlp_b, T)
    return _outproj(x2, ret, moba, gm, w_out.astype(BF16))


def kernel(x, l0_mix_norm, l0_w_in, l0_w_out, l0_gmlp_w, l0_gmlp_b, l0_ffn_norm, l0_w_gate, l0_w_up, l0_w_down, l1_mix_norm, l1_w_in, l1_w_out, l1_gmlp_w, l1_gmlp_b, l1_ffn_norm, l1_router, l1_we_gate, l1_we_up, l1_we_down, final_norm):
    B, S, D = x.shape
    x2 = x.reshape(B * S, D)
    x2 = _mixer(x2, B, S, l0_mix_norm, l0_w_in, l0_w_out, l0_gmlp_w, l0_gmlp_b)
    x2 = _ffn(x2, l0_ffn_norm, l0_w_gate, l0_w_up, l0_w_down)
    x2 = _mixer(x2, B, S, l1_mix_norm, l1_w_in, l1_w_out, l1_gmlp_w, l1_gmlp_b)
    x2 = _moe(x2, l1_ffn_norm, l1_router, l1_we_gate, l1_we_up, l1_we_down, final_norm)
    return x2.reshape(B, S, D)
```

```python
import functools
import math

import jax
import jax.numpy as jnp
import numpy as np
from jax import lax
from jax.experimental import pallas as pl
from jax.experimental.pallas import tpu as pltpu

F32 = jnp.float32
BF16 = jnp.bfloat16

HEAD_DIM = 64
RET_HEADS = 6
MOBA_HEADS = 6
GMLP_GROUPS = 4
RET_WIDTH = RET_HEADS * HEAD_DIM
MOBA_WIDTH = MOBA_HEADS * HEAD_DIM
GMLP_WIDTH = GMLP_GROUPS * HEAD_DIM
RET_CHUNK = 128
MOBA_BLOCK = 256
MOBA_TOPK = 3
GMLP_CHUNK = 128
N_EXPERTS = 8
NORM_EPS = 1e-6
GN_EPS = 1e-5

LANES = 128
HEAD_PAIRS = RET_HEADS // 2
VMEM_LIMIT = 56 * 1024 * 1024
FFN_TOKENS = 2048
FFN_COLS = 512
FFN_CHUNK_ROWS = 512

COL_GU, COL_GV = 0, 2
COL_RQ, COL_RK, COL_RV, COL_RG = 4, 7, 10, 13
COL_MQ, COL_MK, COL_MV = 16, 19, 22
IN_WIDTH = 25 * LANES

NEG_INF = float("-inf")


def _params(*sem):
    return pltpu.CompilerParams(dimension_semantics=sem, vmem_limit_bytes=VMEM_LIMIT)


def _dot(a, b):
    return jnp.dot(a, b, preferred_element_type=F32)


def _dot_nt(a, b):
    return lax.dot_general(a, b, (((1,), (1,)), ((), ())), preferred_element_type=F32)


def _dot_tn(a, b):
    return lax.dot_general(a, b, (((0,), (0,)), ((), ())), preferred_element_type=F32)


def _rmsnorm(x, g):
    ms = jnp.mean(x * x, axis=-1, keepdims=True)
    return (x * lax.rsqrt(ms + NORM_EPS)) * g


def _silu(x):
    return x / (1.0 + jnp.exp(-x))


def _low_half(shape):
    return lax.broadcasted_iota(jnp.int32, shape, len(shape) - 1) < HEAD_DIM


def _pair_mean(x, low):
    s_lo = jnp.sum(jnp.where(low, x, 0.0), axis=-1, keepdims=True)
    s_hi = jnp.sum(jnp.where(low, 0.0, x), axis=-1, keepdims=True)
    return jnp.where(low, s_lo, s_hi) * (1.0 / HEAD_DIM)


def _proj_kernel(x_ref, g_ref, w_ref, o_ref):
    h = _rmsnorm(x_ref[...], g_ref[...]).astype(BF16)
    o_ref[...] = _dot(h, w_ref[...]).astype(o_ref.dtype)


def _proj(x2, g, w_in_bf16, tm=512):
    T, D = x2.shape
    N = w_in_bf16.shape[1]
    return pl.pallas_call(
        _proj_kernel,
        grid=(T // tm,),
        in_specs=[pl.BlockSpec((tm, D), lambda i: (i, 0)),
                  pl.BlockSpec((1, D), lambda i: (0, 0)),
                  pl.BlockSpec((D, N), lambda i: (0, 0))],
        out_specs=pl.BlockSpec((tm, N), lambda i: (i, 0)),
        out_shape=jax.ShapeDtypeStruct((T, N), BF16),
        compiler_params=_params("parallel"),
        name="proj",
    )(x2, g.reshape(1, D), w_in_bf16)


def _retention_tables():
    C, H, D = RET_CHUNK, RET_HEADS, HEAD_DIM
    log_g = np.log(1.0 - 2.0 ** (-5.0 - np.arange(H, dtype=np.float64)))
    pos = np.arange(C, dtype=np.float64)
    diff = pos[:, None] - pos[None, :]
    decay = np.where(diff >= 0, np.exp(log_g[:, None, None] * np.maximum(diff, 0.0)), 0.0)
    decay = decay * D ** -0.5
    zeta = np.exp(log_g[:, None] * (C - 1 - pos)) * D ** -0.5
    xi = np.exp(log_g[:, None] * (pos + 1))
    cdec = np.exp(log_g * C)

    def lanes(t):
        return np.repeat(t.reshape(HEAD_PAIRS, 2, C).transpose(0, 2, 1), D, axis=2)

    cd = np.repeat(cdec.reshape(HEAD_PAIRS, 1, 2), D, axis=2)
    f = lambda a: jnp.asarray(a, F32)
    return f(decay.reshape(HEAD_PAIRS, 2, C, C)), f(lanes(zeta)), f(lanes(xi)), f(cd)


def _retention_kernel(q_ref, k_ref, v_ref, g_ref, dec_ref, zeta_ref, xi_ref, cd_ref, o_ref,
                      state_ref, *, chunks):
    C = RET_CHUNK

    @pl.when(pl.program_id(2) == 0)
    def _():
        state_ref[...] = jnp.zeros_like(state_ref)

    low = _low_half((C, LANES))
    r = lax.broadcasted_iota(jnp.int32, (LANES, LANES), 0) < HEAD_DIM
    c = lax.broadcasted_iota(jnp.int32, (LANES, LANES), 1) < HEAD_DIM
    same_head = r == c
    zeta = zeta_ref[0]
    xi = xi_ref[0]
    cd = cd_ref[0]
    zero = jnp.zeros((), BF16)

    for ci in range(chunks):
        rows = pl.ds(ci * C, C)
        q = q_ref[rows, :]
        k = k_ref[rows, :]
        v = v_ref[rows, :]
        state = state_ref[...]
        p0 = (_dot_nt(jnp.where(low, q, zero), k) * dec_ref[0, 0]).astype(BF16)
        p1 = (_dot_nt(jnp.where(low, zero, q), k) * dec_ref[0, 1]).astype(BF16)
        intra = _dot(p0, jnp.where(low, v, zero)) + _dot(p1, jnp.where(low, zero, v))
        cross = _dot((q.astype(F32) * xi).astype(BF16), state.astype(BF16))
        o = intra + cross
        kz = (k.astype(F32) * zeta).astype(BF16)
        kv = _dot_tn(kz, v)
        state_ref[...] = state * cd + jnp.where(same_head, kv, 0.0)
        mu = _pair_mean(o, low)
        d = o - mu
        var = _pair_mean(d * d, low)
        on = d * lax.rsqrt(var + GN_EPS)
        o_ref[rows, :] = (on * _silu(g_ref[rows, :].astype(F32))).astype(o_ref.dtype)


def _retention(proj, B, S, ts=1024):
    T = B * S
    nst = S // ts
    dec, zeta, xi, cd = _retention_tables()
    C = RET_CHUNK

    def col(c0):
        return pl.BlockSpec((ts, LANES), lambda b, p, s: (b * nst + s, c0 + p))

    return pl.pallas_call(
        functools.partial(_retention_kernel, chunks=ts // C),
        grid=(B, HEAD_PAIRS, nst),
        in_specs=[col(COL_RQ), col(COL_RK), col(COL_RV), col(COL_RG),
                  pl.BlockSpec((1, 2, C, C), lambda b, p, s: (p, 0, 0, 0)),
                  pl.BlockSpec((1, C, LANES), lambda b, p, s: (p, 0, 0)),
                  pl.BlockSpec((1, C, LANES), lambda b, p, s: (p, 0, 0)),
                  pl.BlockSpec((1, 1, LANES), lambda b, p, s: (p, 0, 0))],
        out_specs=pl.BlockSpec((ts, LANES), lambda b, p, s: (b * nst + s, p)),
        out_shape=jax.ShapeDtypeStruct((T, RET_WIDTH), BF16),
        scratch_shapes=[pltpu.VMEM((LANES, LANES), F32)],
        compiler_params=_params("parallel", "parallel", "arbitrary"),
        name="retention",
    )(proj, proj, proj, proj, dec, zeta, xi, cd)


AUG_SLOPE_ROWS = 6
AUG_BIAS_ROW0 = 16
MASK_BIAS = -1e30


def _moba_kernel(slope_ref, q_ref, k_ref, v_ref, o_ref,
                 ka_ref, vt_ref, kmean_ref, qa_ref, m_ref, l_ref, acc_ref, *, nblocks):
    BS = MOBA_BLOCK
    p = pl.program_id(1)
    qi = pl.program_id(2)

    @pl.when(qi == 0)
    def _():
        lane = lax.broadcasted_iota(jnp.int32, (BS, LANES), 1)
        low = lane < HEAD_DIM
        offs = lax.broadcasted_iota(jnp.int32, (BS, LANES), 0).astype(F32)
        kmean_ref[...] = jnp.zeros_like(kmean_ref)
        for jb in range(nblocks):
            kb = k_ref[jb * BS:(jb + 1) * BS, :]
            for hh in range(2):
                a = lane - HEAD_DIM if hh == 0 else lane
                aug = jnp.where(a < 3, offs,
                                jnp.where(a < AUG_SLOPE_ROWS, float(jb * BS),
                                          jnp.where(a == AUG_BIAS_ROW0 + jb, 1.0, 0.0))).astype(BF16)
                ka_ref[hh, jb] = jnp.where(low, kb, aug) if hh == 0 else jnp.where(low, aug, kb)
            kmean_ref[jb:jb + 1, :] = jnp.mean(kb.astype(F32), axis=0, keepdims=True)
            vt_ref[jb] = v_ref[jb * BS:(jb + 1) * BS, :].astype(F32).T.astype(BF16)

    q_t = (q_ref[...].astype(F32) * HEAD_DIM ** -0.5).T
    kmean = kmean_ref[...].astype(BF16)
    r16 = lax.broadcasted_iota(jnp.int32, (AUG_BIAS_ROW0, BS), 0)
    key_off = lax.broadcasted_iota(jnp.int32, (BS, BS), 0)
    qry_off = lax.broadcasted_iota(jnp.int32, (BS, BS), 1)
    causal = key_off <= qry_off
    zeros_h = jnp.zeros((HEAD_DIM, BS), F32)
    pad = jnp.zeros((HEAD_DIM - 2 * AUG_BIAS_ROW0, BS), F32)

    piece = r16 % 3
    qhs, srows = [], []
    for hh in range(2):
        qhs.append(q_t[hh * HEAD_DIM:(hh + 1) * HEAD_DIM, :])
        base = 3 * (2 * p + hh)
        s1, s2, s3 = slope_ref[base], slope_ref[base + 1], slope_ref[base + 2]
        srows.append(jnp.where(r16 < AUG_SLOPE_ROWS,
                               jnp.where(piece == 0, s1, jnp.where(piece == 1, s2, s3)), 0.0))

    def with_aug(hh, bias_rows):
        aug = jnp.concatenate([srows[hh], bias_rows, pad], axis=0)
        return jnp.concatenate([qhs[hh], aug] if hh == 0 else [aug, qhs[hh]], axis=0).astype(BF16)

    qm_both = jnp.concatenate([jnp.concatenate([qhs[0], zeros_h], axis=0),
                               jnp.concatenate([zeros_h, qhs[1]], axis=0)], axis=1).astype(BF16)
    bs_both = _dot(kmean, qm_both)[0:AUG_BIAS_ROW0, :]
    own_bias = jnp.where(r16 == qi, 0.0, MASK_BIAS)
    s_own = [jnp.where(causal, _dot(ka_ref[hh, qi], with_aug(hh, own_bias)), NEG_INF)
             for hh in range(2)]

    for hh in range(2):
        sc = jnp.where(r16 < qi, bs_both[:, hh * BS:(hh + 1) * BS], NEG_INF)
        bias = jnp.full((AUG_BIAS_ROW0, BS), MASK_BIAS, F32)
        for _ in range(MOBA_TOPK):
            mx = jnp.max(sc, axis=0, keepdims=True)
            idx = jnp.min(jnp.where(sc == mx, r16, AUG_BIAS_ROW0), axis=0, keepdims=True)
            pick = (r16 == idx) & (mx > NEG_INF)
            bias = jnp.where(pick, 0.0, bias)
            sc = jnp.where(pick, NEG_INF, sc)
        qa_ref[hh] = with_aug(hh, bias)

    n_pairs = (qi + 1) // 2

    def pair_scores(jj):
        j0 = 2 * jj
        out = []
        for hh in range(2):
            ka = jnp.concatenate([ka_ref[hh, j0], ka_ref[hh, j0 + 1]], axis=0)
            out.append(_dot(ka, qa_ref[hh]))
        return tuple(out)

    first = pair_scores(0)

    for hh in range(2):
        s = s_own[hh]
        m = jnp.max(s, axis=0, keepdims=True)
        e = jnp.exp(s - m)
        m_ref[hh] = m
        l_ref[hh] = jnp.sum(e, axis=0, keepdims=True)
        acc_ref[hh] = _dot(vt_ref[qi, hh * HEAD_DIM:(hh + 1) * HEAD_DIM, :], e.astype(BF16))

    def body(jj, ss):
        nxt = pair_scores(jnp.minimum(jj + 1, n_pairs - 1))
        j0 = 2 * jj
        for hh in range(2):
            s = ss[hh]
            m_old = m_ref[hh]
            m_new = jnp.maximum(m_old, jnp.max(s, axis=0, keepdims=True))
            alpha = jnp.exp(m_old - m_new)
            e = jnp.exp(s - m_new)
            m_ref[hh] = m_new
            l_ref[hh] = l_ref[hh] * alpha + jnp.sum(e, axis=0, keepdims=True)
            rows = slice(hh * HEAD_DIM, (hh + 1) * HEAD_DIM)
            vt = jnp.concatenate([vt_ref[j0, rows, :], vt_ref[j0 + 1, rows, :]], axis=1)
            acc_ref[hh] = acc_ref[hh] * alpha + _dot(vt, e.astype(BF16))
        return nxt

    lax.fori_loop(0, n_pairs, body, first)
    out_t = jnp.concatenate([acc_ref[0] / l_ref[0], acc_ref[1] / l_ref[1]], axis=0)
    o_ref[...] = out_t.T.astype(o_ref.dtype)


def _moba(proj, B, S):
    T = B * S
    BS = MOBA_BLOCK
    nb = S // BS
    slopes = jnp.asarray(2.0 ** (-8.0 * np.arange(1, MOBA_HEADS + 1) / MOBA_HEADS), F32)
    s1 = slopes.astype(BF16).astype(F32)
    s2 = (slopes - s1).astype(BF16).astype(F32)
    s3 = (slopes - s1 - s2).astype(BF16).astype(F32)
    slopes = jnp.stack([s1, s2, s3], axis=1).reshape(-1)
    return pl.pallas_call(
        functools.partial(_moba_kernel, nblocks=nb),
        grid=(B, HEAD_PAIRS, nb),
        in_specs=[pl.BlockSpec(memory_space=pltpu.SMEM),
                  pl.BlockSpec((BS, LANES), lambda b, p, i: (b * nb + i, COL_MQ + p)),
                  pl.BlockSpec((S, LANES), lambda b, p, i: (b, COL_MK + p)),
                  pl.BlockSpec((S, LANES), lambda b, p, i: (b, COL_MV + p))],
        out_specs=pl.BlockSpec((BS, LANES), lambda b, p, i: (b * nb + i, p)),
        out_shape=jax.ShapeDtypeStruct((T, MOBA_WIDTH), BF16),
        scratch_shapes=[pltpu.VMEM((2, nb, BS, LANES), BF16),
                        pltpu.VMEM((nb, LANES, BS), BF16),
                        pltpu.VMEM((LANES, LANES), F32),
                        pltpu.VMEM((2, LANES, BS), BF16),
                        pltpu.VMEM((2, 1, BS), F32),
                        pltpu.VMEM((2, 1, BS), F32),
                        pltpu.VMEM((2, HEAD_DIM, BS), F32)],
        compiler_params=_params("parallel", "parallel", "arbitrary"),
        name="moba",
    )(slopes, proj, proj, proj)


def _gmlp_kernel(u_ref, v_ref, w_ref, b_ref, o_ref, *, chunks):
    C = GMLP_CHUNK
    low = _low_half((C, LANES))
    r = lax.broadcasted_iota(jnp.int32, (C, 2 * C), 0)
    c = lax.broadcasted_iota(jnp.int32, (C, 2 * C), 1)
    tril = jnp.where(c >= C, c - C, c) <= r
    ws = [jnp.where(tril, w_ref[pr], 0.0).astype(BF16) for pr in range(2)]
    for ci in range(chunks):
        rows = pl.ds(ci * C, C)
        u = jax.nn.gelu(u_ref[rows, :].astype(F32))
        vf = jax.nn.gelu(v_ref[rows, :].astype(F32))
        mu = jnp.mean(vf, axis=-1, keepdims=True)
        d = vf - mu
        var = jnp.mean(d * d, axis=-1, keepdims=True)
        vn = d * lax.rsqrt(var + GN_EPS)
        mixed = []
        for pr in range(2):
            vp = vn[:, pr * LANES:(pr + 1) * LANES]
            stacked = jnp.concatenate([jnp.where(low, vp, 0.0), jnp.where(low, 0.0, vp)], axis=0)
            mixed.append(_dot(ws[pr], stacked.astype(BF16)))
        mixed = jnp.concatenate(mixed, axis=1) + b_ref[...]
        o_ref[rows, :] = (u * mixed).astype(o_ref.dtype)


def _gmlp(proj, gmlp_w, gmlp_b, T, ts=1024):
    C, G = GMLP_CHUNK, GMLP_GROUPS
    w_cat = gmlp_w.astype(F32).reshape(G // 2, 2, C, C).transpose(0, 2, 1, 3).reshape(G // 2, C, 2 * C)
    b_tab = jnp.repeat(gmlp_b.astype(F32).T, HEAD_DIM, axis=1)
    W = GMLP_WIDTH
    return pl.pallas_call(
        functools.partial(_gmlp_kernel, chunks=ts // C),
        grid=(T // ts,),
        in_specs=[pl.BlockSpec((ts, W), lambda i: (i, COL_GU * LANES // W)),
                  pl.BlockSpec((ts, W), lambda i: (i, COL_GV * LANES // W)),
                  pl.BlockSpec((G // 2, C, 2 * C), lambda i: (0, 0, 0)),
                  pl.BlockSpec((C, W), lambda i: (0, 0))],
        out_specs=pl.BlockSpec((ts, W), lambda i: (i, 0)),
        out_shape=jax.ShapeDtypeStruct((T, W), BF16),
        compiler_params=_params("parallel"),
        name="gmlp",
    )(proj, proj, w_cat, b_tab)


def _outproj_kernel(x_ref, r_ref, m_ref, g_ref, w_ref, o_ref):
    mix = jnp.concatenate([r_ref[...], m_ref[...], g_ref[...]], axis=1)
    o_ref[...] = x_ref[...] + _dot(mix, w_ref[...])


def _outproj(x2, ret, moba, gm, w_out_bf16, tm=1024):
    T, D = x2.shape
    row = lambda w: pl.BlockSpec((tm, w), lambda i: (i, 0))
    return pl.pallas_call(
        _outproj_kernel,
        grid=(T // tm,),
        in_specs=[row(D), row(RET_WIDTH), row(MOBA_WIDTH), row(GMLP_WIDTH),
                  pl.BlockSpec(w_out_bf16.shape, lambda i: (0, 0))],
        out_specs=row(D),
        out_shape=jax.ShapeDtypeStruct((T, D), F32),
        compiler_params=_params("parallel"),
        name="outproj",
    )(x2, ret, moba, gm, w_out_bf16)


def _swiglu_accumulate(h_ref, wg, wu, wd, o_ref, scale_ref):
    tm = h_ref.shape[0]
    tf = wg.shape[1]
    rc = FFN_CHUNK_ROWS
    n_chunks = tm // rc
    wg, wu, wd = wg.astype(BF16), wu.astype(BF16), wd.astype(BF16)

    def gate_up(c):
        h = h_ref[c * rc:(c + 1) * rc, :]
        return _dot(h, wg), _dot(h, wu)

    cur = gate_up(0)
    for c in range(n_chunks):
        nxt = gate_up(c + 1) if c + 1 < n_chunks else None
        rows = slice(c * rc, (c + 1) * rc)
        a = _silu(cur[0]) * cur[1]
        if scale_ref is not None:
            a = a * jnp.concatenate([scale_ref[rows, :]] * (tf // LANES), axis=1)
        o_ref[rows, :] += _dot(a.astype(BF16), wd)
        cur = nxt


def _ffn_kernel(x_ref, g_ref, wg_ref, wu_ref, wd_ref, o_ref, h_ref):
    f = pl.program_id(1)

    @pl.when(f == 0)
    def _():
        x = x_ref[...]
        h_ref[...] = _rmsnorm(x, g_ref[...]).astype(BF16)
        o_ref[...] = x

    _swiglu_accumulate(h_ref, wg_ref[...], wu_ref[...], wd_ref[...], o_ref, None)


def _single(block_shape, index_map):
    return pl.BlockSpec(block_shape, index_map, pipeline_mode=pl.Buffered(1))


def _ffn(x2, g, wg, wu, wd, tm=FFN_TOKENS // 2, tf=FFN_COLS):
    T, D = x2.shape
    F = wg.shape[1]
    return pl.pallas_call(
        _ffn_kernel,
        grid=(T // tm, F // tf),
        in_specs=[pl.BlockSpec((tm, D), lambda i, f: (i, 0)),
                  pl.BlockSpec((1, D), lambda i, f: (0, 0)),
                  pl.BlockSpec((D, tf), lambda i, f: (0, f)),
                  pl.BlockSpec((D, tf), lambda i, f: (0, f)),
                  pl.BlockSpec((tf, D), lambda i, f: (f, 0))],
        out_specs=pl.BlockSpec((tm, D), lambda i, f: (i, 0)),
        out_shape=jax.ShapeDtypeStruct((T, D), F32),
        scratch_shapes=[pltpu.VMEM((tm, D), BF16)],
        compiler_params=_params("parallel", "arbitrary"),
        name="ffn",
    )(x2, g.reshape(1, D), wg, wu, wd)


def _moe_kernel(x_ref, g_ref, r_ref, wg_ref, wu_ref, wd_ref, fg_ref, o_ref,
                h_ref, gates_ref, gcol_ref):
    e = pl.program_id(1)
    f = pl.program_id(2)
    tm = x_ref.shape[0]
    tf = wg_ref.shape[2]
    lane = lax.broadcasted_iota(jnp.int32, (tm, LANES), 1)

    @pl.when((e == 0) & (f == 0))
    def _():
        x = x_ref[...]
        h = _rmsnorm(x, g_ref[...]).astype(BF16)
        h_ref[...] = h
        o_ref[...] = x
        logits = jnp.where(lane < N_EXPERTS, _dot(h, r_ref[...]), NEG_INF)
        m1 = jnp.max(logits, axis=1, keepdims=True)
        i1 = jnp.min(jnp.where(logits == m1, lane, LANES), axis=1, keepdims=True)
        rest = jnp.where(lane == i1, NEG_INF, logits)
        m2 = jnp.max(rest, axis=1, keepdims=True)
        i2 = jnp.min(jnp.where(rest == m2, lane, LANES), axis=1, keepdims=True)
        e2 = jnp.exp(m2 - m1)
        g1 = 1.0 / (1.0 + e2)
        g2 = e2 / (1.0 + e2)
        gates_ref[...] = jnp.where(lane == i1, g1, 0.0) + jnp.where(lane == i2, g2, 0.0)

    @pl.when(f == 0)
    def _():
        col = jnp.sum(jnp.where(lane == e, gates_ref[...], 0.0), axis=1, keepdims=True)
        gcol_ref[...] = jnp.broadcast_to(col, (tm, LANES))

    _swiglu_accumulate(h_ref, wg_ref[0], wu_ref[0], wd_ref[0], o_ref, gcol_ref)

    @pl.when((e == pl.num_programs(1) - 1) & (f == pl.num_programs(2) - 1))
    def _():
        o_ref[...] = _rmsnorm(o_ref[...], fg_ref[...])


def _moe(x2, g, router, wg, wu, wd, final_g, tm=FFN_TOKENS, tf=FFN_COLS):
    T, D = x2.shape
    E, _, F = wg.shape
    r_pad = jnp.zeros((D, LANES), BF16).at[:, :E].set(router.astype(BF16))
    return pl.pallas_call(
        _moe_kernel,
        grid=(T // tm, E, F // tf),
        in_specs=[_single((tm, D), lambda i, e, f: (i, 0)),
                  pl.BlockSpec((1, D), lambda i, e, f: (0, 0)),
                  pl.BlockSpec((D, LANES), lambda i, e, f: (0, 0)),
                  pl.BlockSpec((1, D, tf), lambda i, e, f: (e, 0, f)),
                  pl.BlockSpec((1, D, tf), lambda i, e, f: (e, 0, f)),
                  pl.BlockSpec((1, tf, D), lambda i, e, f: (e, f, 0)),
                  pl.BlockSpec((1, D), lambda i, e, f: (0, 0))],
        out_specs=_single((tm, D), lambda i, e, f: (i, 0)),
        out_shape=jax.ShapeDtypeStruct((T, D), F32),
        scratch_shapes=[pltpu.VMEM((tm, D), BF16),
                        pltpu.VMEM((tm, LANES), F32), pltpu.VMEM((tm, LANES), F32)],
        compiler_params=_params("parallel", "arbitrary", "arbitrary"),
        name="moe",
    )(x2, g.reshape(1, D), r_pad, wg, wu, wd, final_g.reshape(1, D))


def _permute_w_in(w_in):
    n_ret, n_moba = 4 * RET_WIDTH, 3 * MOBA_WIDTH
    gm = w_in[:, n_ret + n_moba:]
    return jnp.concatenate([gm, w_in[:, :n_ret + n_moba]], axis=1).astype(BF16)


def _mixer(x2, B, S, norm_g, w_in, w_out, gmlp_w, gmlp_b):
    T = B * S
    proj = _proj(x2, norm_g, _permute_w_in(w_in))
    ret = _retention(proj, B, S)
    moba = _moba(proj, B, S)
    gm = _gmlp(proj, gmlp_w, gmlp_b, T)
    return _outproj(x2, ret, moba, gm, w_out.astype(BF16))


def kernel(x, l0_mix_norm, l0_w_in, l0_w_out, l0_gmlp_w, l0_gmlp_b, l0_ffn_norm, l0_w_gate, l0_w_up, l0_w_down, l1_mix_norm, l1_w_in, l1_w_out, l1_gmlp_w, l1_gmlp_b, l1_ffn_norm, l1_router, l1_we_gate, l1_we_up, l1_we_down, final_norm):
    B, S, D = x.shape
    x2 = x.reshape(B * S, D)
    x2 = _mixer(x2, B, S, l0_mix_norm, l0_w_in, l0_w_out, l0_gmlp_w, l0_gmlp_b)
    x2 = _ffn(x2, l0_ffn_norm, l0_w_gate, l0_w_up, l0_w_down)
    x2 = _mixer(x2, B, S, l1_mix_norm, l1_w_in, l1_w_out, l1_gmlp_w, l1_gmlp_b)
    x2 = _moe(x2, l1_ffn_norm, l1_router, l1_we_gate, l1_we_up, l1_we_down, final_norm)
    return x2.reshape(B, S, D)
```
